```python
import math
import jax
import jax.numpy as jnp
from jax import lax
import numpy as np

D_MODEL = 2048
BATCH = 2
SEQ = 16384
DEPTH = 1

GRID_W = 64
CTX_LEN = 256
N_ML_HEADS = 8
ML_DQK = 64
ML_DV = 128
ML_CONV_W = 5
ML_CHUNK = 128
N_DA_HEADS = 8
DA_DQK = 64
DA_DV = 128
Q_BLOCK = 128
ML_WIDTH = N_ML_HEADS * ML_DV
DA_WIDTH = N_DA_HEADS * DA_DV
MIX_WIDTH = ML_WIDTH + DA_WIDTH
N_EXPERTS = 16
EC_CAPACITY_FACTOR = 2
D_EXPERT = 4096
ROPE_BASE = 10000.0
LN_EPS = 1e-5
DEEPNORM_ALPHA = (2.0 * DEPTH) ** 0.25
DEEPNORM_BETA = (8.0 * DEPTH) ** -0.25

QK_ML_COLS = 2 * N_ML_HEADS * ML_DQK
V_ML_COLS = ML_WIDTH
O_ML_COLS = ML_WIDTH
GATE_COLS = 4 * N_ML_HEADS
QK_DA_COLS = N_DA_HEADS * 2 * DA_DQK
V_DA_COLS = DA_WIDTH
OFF_QK_ML = 0
OFF_V_ML = OFF_QK_ML + QK_ML_COLS
OFF_O_ML = OFF_V_ML + V_ML_COLS
OFF_GATES = OFF_O_ML + O_ML_COLS
OFF_Q_DA = OFF_GATES + GATE_COLS
OFF_K_DA = OFF_Q_DA + QK_DA_COLS
OFF_V_DA = OFF_K_DA + QK_DA_COLS
IN_COLS = OFF_V_DA + V_DA_COLS

kernel_name = 'hybrid_mlstm_diffattn_ec_moe_dit_layer'


def _layer_norm(x, w=None, b=None):
    xf = x.astype(jnp.float32)
    mu = jnp.mean(xf, -1, keepdims=True)
    var = jnp.mean(jnp.square(xf - mu), -1, keepdims=True)
    y = (xf - mu) * lax.rsqrt(var + LN_EPS)
    if w is not None:
        y = y * w.astype(jnp.float32) + b.astype(jnp.float32)
    return y.astype(x.dtype)


def _modulate(h, shift, scale):
    return _layer_norm(h) * (1 + scale) + shift


def _dwconv_centred(x, w, b):
    k = w.shape[0]
    y = lax.conv_general_dilated(x, w[:, None, :], window_strides=(1,), padding=[(k // 2, k // 2)],
                                 dimension_numbers=('NWC', 'WIO', 'NWC'), feature_group_count=x.shape[-1])
    return y + b


def _axial_rope_tables(pos_row, pos_col, dim):
    n_freq = dim // 4
    inv = ROPE_BASE ** (-jnp.arange(n_freq, dtype=jnp.float32) / n_freq)
    ang = jnp.concatenate([pos_row.astype(jnp.float32)[:, None] * inv,
                           pos_col.astype(jnp.float32)[:, None] * inv], -1)
    return jnp.cos(ang), jnp.sin(ang)


def _apply_rope(x, cos, sin):
    half = x.shape[-1] // 2
    xf = x.astype(jnp.float32)
    x1, x2 = xf[..., :half], xf[..., half:]
    c = cos[None, :, None, None, :]
    s = sin[None, :, None, None, :]
    return jnp.concatenate([x1 * c - x2 * s, x1 * s + x2 * c], -1).astype(x.dtype)


def _zero_state(batch):
    return (jnp.zeros((batch, N_ML_HEADS, ML_DQK, ML_DV), jnp.float32),
            jnp.zeros((batch, N_ML_HEADS, ML_DQK), jnp.float32),
            jnp.zeros((batch, N_ML_HEADS), jnp.float32))


def _mlstm_inputs(p, conv_w, conv_b, b_gates):
    B, N, _ = p.shape
    qk = jax.nn.silu(_dwconv_centred(p[..., OFF_QK_ML:OFF_V_ML], conv_w, conv_b))
    half = QK_ML_COLS // 2
    q = qk[..., :half].reshape(B, N, N_ML_HEADS, ML_DQK).transpose(0, 2, 1, 3) * ML_DQK ** -0.5
    k = qk[..., half:].reshape(B, N, N_ML_HEADS, ML_DQK).transpose(0, 2, 1, 3)
    v = p[..., OFF_V_ML:OFF_O_ML].reshape(B, N, N_ML_HEADS, ML_DV).transpose(0, 2, 1, 3)
    o = p[..., OFF_O_ML:OFF_GATES]
    g = (p[..., OFF_GATES:OFF_Q_DA] + b_gates).astype(jnp.float32)
    g = g.reshape(B, N, 4, N_ML_HEADS).transpose(2, 0, 3, 1)
    gates_f = (g[0], jax.nn.log_sigmoid(g[1]))
    gates_b = (g[2], jax.nn.log_sigmoid(g[3]))
    return q, k, v, o, gates_f, gates_b


def _mlstm_chunkwise(q, k, v, log_i, log_f, state):
    q, k, v = (a.astype(jnp.float32) for a in (q, k, v))
    B, H, N, _ = q.shape
    L = ML_CHUNK
    nc = N // L

    def to_chunks(a):
        return jnp.moveaxis(a.reshape(B, H, nc, L, *a.shape[3:]), 2, 0)

    mask = jnp.tril(jnp.ones((L, L), dtype=bool))

    def step(carry, inp):
        C, n, m = carry
        qb, kb, vb, ib, fb = inp
        b = jnp.cumsum(fb, -1)
        D = jnp.where(mask, b[..., :, None] - b[..., None, :] + ib[..., None, :], -jnp.inf)
        m_inter = b + m[..., None]
        m_t = jnp.maximum(m_inter, jnp.max(D, -1))
        w_inter = jnp.exp(m_inter - m_t)
        W = jnp.exp(D - m_t[..., None]) * jnp.einsum('bhtd,bhsd->bhts', qb, kb)
        num = w_inter[..., None] * jnp.einsum('bhtd,bhde->bhte', qb, C) + jnp.einsum('bhts,bhse->bhte', W, vb)
        den = w_inter * jnp.einsum('bhtd,bhd->bht', qb, n) + jnp.sum(W, -1)
        h = num / jnp.maximum(jnp.abs(den), jnp.exp(-m_t))[..., None]
        bL = b[..., -1]
        gdec = bL[..., None] - b + ib
        m_new = jnp.maximum(bL + m, jnp.max(gdec, -1))
        a = jnp.exp(bL + m - m_new)
        wg = jnp.exp(gdec - m_new[..., None])
        C_new = a[..., None, None] * C + jnp.einsum('bhs,bhsd,bhse->bhde', wg, kb, vb)
        n_new = a[..., None] * n + jnp.einsum('bhs,bhsd->bhd', wg, kb)
        return (C_new, n_new, m_new), h

    state, hc = lax.scan(step, state, tuple(to_chunks(a) for a in (q, k, v, log_i, log_f)))
    return jnp.moveaxis(hc, 0, 2).reshape(B, H, N, ML_DV), state


def _mlstm_bidir(q, k, v, gates_f, gates_b, state_f, state_b):
    flip = lambda a: jnp.flip(a, axis=2)
    h_f, st_f = _mlstm_chunkwise(q, k, v, gates_f[0], gates_f[1], state_f)
    h_b, st_b = _mlstm_chunkwise(flip(q), flip(k), flip(v), flip(gates_b[0]), flip(gates_b[1]), state_b)
    return h_f + flip(h_b), st_f, st_b


def _mlstm_out(h, o, norm_w):
    B, H, N, dv = h.shape
    hn = _layer_norm(h).transpose(0, 2, 1, 3).reshape(B, N, H * dv)
    return (hn * norm_w.astype(jnp.float32) * jax.nn.sigmoid(o.astype(jnp.float32))).astype(o.dtype)


def _da_inputs(p):
    B, N, _ = p.shape
    q = p[..., OFF_Q_DA:OFF_K_DA].reshape(B, N, N_DA_HEADS, 2, DA_DQK)
    k = p[..., OFF_K_DA:OFF_V_DA].reshape(B, N, N_DA_HEADS, 2, DA_DQK)
    v = p[..., OFF_V_DA:IN_COLS].reshape(B, N, N_DA_HEADS, DA_DV)
    return q, k, v


def _diff_attention(q, k, v, lam):
    B, N, H, _, dk = q.shape
    nb = N // Q_BLOCK
    q_blocks = jnp.moveaxis(q.reshape(B, nb, Q_BLOCK, H, 2, dk), 1, 0)
    scale = dk ** -0.5

    def block(qb):
        s = jnp.einsum('bqhcd,bkhcd->bhcqk', qb, k).astype(jnp.float32) * scale
        pr = jax.nn.softmax(s, axis=-1)
        a = (pr[:, :, 0] - lam * pr[:, :, 1]).astype(v.dtype)
        return jnp.einsum('bhqk,bkhe->bqhe', a, v)

    o = lax.map(block, q_blocks)
    return jnp.moveaxis(o, 0, 1).reshape(B, N, H, v.shape[-1])


def _da_out(o, norm_w, lambda_init):
    B, N, H, dv = o.shape
    of = o.astype(jnp.float32)
    of = of * lax.rsqrt(jnp.mean(jnp.square(of), -1, keepdims=True) + LN_EPS)
    return (of.reshape(B, N, H * dv) * norm_w.astype(jnp.float32) * (1.0 - lambda_init)).astype(o.dtype)


def _merge_heads(h_ml, o_ml, o_da, ml_norm_w, da_norm_w, lambda_init, w_out):
    y = jnp.concatenate([_mlstm_out(h_ml, o_ml, ml_norm_w), _da_out(o_da, da_norm_w, lambda_init)], -1)
    return y @ w_out


def _expert_choice_ffn(u, w_router, w_gate, w_up, w_down):
    B, N, D = u.shape
    cap = EC_CAPACITY_FACTOR * N // N_EXPERTS
    aff = jax.nn.softmax((u @ w_router).astype(jnp.float32), axis=-1)
    g, idx = lax.top_k(jnp.swapaxes(aff, 1, 2), cap)

    def expert(args):
        idx_e, g_e, wg, wu, wd = args
        xs = jax.vmap(lambda ub, ib: ub[ib])(u, idx_e)
        hid = jax.nn.silu(xs @ wg) * (xs @ wu)
        return (hid @ wd) * g_e[..., None].astype(u.dtype)

    ys = lax.map(expert, (jnp.swapaxes(idx, 0, 1), jnp.swapaxes(g, 0, 1), w_gate, w_up, w_down))
    ys = jnp.moveaxis(ys, 0, 1).reshape(B, -1, D)
    idx_flat = idx.reshape(B, -1)
    return jax.vmap(lambda i, y: jnp.zeros((N, D), y.dtype).at[i].add(y))(idx_flat, ys)


def setup_inputs(seed: int = 0) -> dict:
    key = jax.random.key(seed)
    ks = jax.random.split(key, 26)
    f32 = jnp.float32
    nrm = lambda k, shape, s: jax.random.normal(k, shape, f32) * s
    H = N_ML_HEADS
    b_gates = jnp.concatenate([nrm(ks[8], (DEPTH, H), 0.1),
                               3.0 + 3.0 * jax.random.uniform(ks[9], (DEPTH, H), f32),
                               nrm(ks[10], (DEPTH, H), 0.1),
                               3.0 + 3.0 * jax.random.uniform(ks[11], (DEPTH, H), f32)], -1)
    return {
        'x': nrm(ks[0], (BATCH, SEQ, D_MODEL), 1.0),
        'c': nrm(ks[1], (BATCH, D_MODEL), 1.0),
        'ctx': nrm(ks[2], (BATCH, CTX_LEN, D_MODEL), 1.0),
        'c_ctx': nrm(ks[3], (D_MODEL,), 1.0),
        'w_ada': nrm(ks[4], (DEPTH, D_MODEL, 6 * D_MODEL), 0.5 * D_MODEL ** -0.5),
        'b_ada': nrm(ks[5], (DEPTH, 6 * D_MODEL), 0.02),
        'w_in': nrm(ks[6], (DEPTH, D_MODEL, IN_COLS), D_MODEL ** -0.5),
        'b_gates': b_gates,
        'ml_conv_w': nrm(ks[7], (DEPTH, ML_CONV_W, QK_ML_COLS), ML_CONV_W ** -0.5),
        'ml_conv_b': nrm(ks[12], (DEPTH, QK_ML_COLS), 0.02),
        'ml_norm_w': 1.0 + nrm(ks[13], (DEPTH, ML_WIDTH), 0.02),
        'da_lambda': nrm(ks[14], (DEPTH, 4, DA_DQK), 0.1),
        'da_norm_w': 1.0 + nrm(ks[15], (DEPTH, DA_WIDTH), 0.02),
        'w_out': nrm(ks[16], (DEPTH, MIX_WIDTH, D_MODEL), DEEPNORM_BETA * MIX_WIDTH ** -0.5),
        'ln1_w': 1.0 + nrm(ks[17], (DEPTH, D_MODEL), 0.02),
        'ln1_b': nrm(ks[18], (DEPTH, D_MODEL), 0.02),
        'w_router': nrm(ks[19], (DEPTH, D_MODEL, N_EXPERTS), D_MODEL ** -0.5),
        'w_gate': nrm(ks[20], (DEPTH, N_EXPERTS, D_MODEL, D_EXPERT), D_MODEL ** -0.5),
        'w_up': nrm(ks[21], (DEPTH, N_EXPERTS, D_MODEL, D_EXPERT), D_MODEL ** -0.5),
        'w_down': nrm(ks[22], (DEPTH, N_EXPERTS, D_EXPERT, D_MODEL), DEEPNORM_BETA * D_EXPERT ** -0.5),
        'ln2_w': 1.0 + nrm(ks[23], (DEPTH, D_MODEL), 0.02),
        'ln2_b': nrm(ks[24], (DEPTH, D_MODEL), 0.02),
    }


def reference(x, c, ctx, c_ctx, w_ada, b_ada, w_in, b_gates, ml_conv_w, ml_conv_b, ml_norm_w,
              da_lambda, da_norm_w, w_out, ln1_w, ln1_b, w_router, w_gate, w_up, w_down, ln2_w, ln2_b):
    B, n_lat, _ = x.shape
    ROWS = n_lat // GRID_W
    pos_row = jnp.repeat(jnp.arange(ROWS, dtype=jnp.int32), GRID_W)
    pos_col = jnp.tile(jnp.arange(GRID_W, dtype=jnp.int32), ROWS)
    cos, sin = _axial_rope_tables(pos_row, pos_col, DA_DQK)

    h, h_ctx = x, ctx
    for l in range(DEPTH):
        last = l == DEPTH - 1
        lambda_init = 0.8 - 0.6 * math.exp(-0.3 * l)
        mod_lat = (jax.nn.silu(c) @ w_ada[l] + b_ada[l])[:, None, :]
        mod_ctx = (jax.nn.silu(c_ctx) @ w_ada[l] + b_ada[l])[None, None, :]
        sh1, sc1, g1, sh2, sc2, g2 = jnp.split(mod_lat, 6, axis=-1)
        csh1, csc1, cg1, csh2, csc2, cg2 = jnp.split(mod_ctx, 6, axis=-1)

        p_lat = _modulate(h, sh1, sc1) @ w_in[l]
        p_ctx = _modulate(h_ctx, csh1, csc1) @ w_in[l]

        qc, kc, vc, oc, gfc, gbc = _mlstm_inputs(p_ctx, ml_conv_w[l], ml_conv_b[l], b_gates[l])
        ql, kl, vl, ol, gfl, gbl = _mlstm_inputs(p_lat, ml_conv_w[l], ml_conv_b[l], b_gates[l])
        zero = _zero_state(B)
        hml_c, st_f, st_b = _mlstm_bidir(qc, kc, vc, gfc, gbc, zero, zero)
        hml_l, _, _ = _mlstm_bidir(ql, kl, vl, gfl, gbl, st_f, st_b)

        lq1, lk1, lq2, lk2 = da_lambda[l].astype(jnp.float32)
        lam = jnp.exp(jnp.sum(lq1 * lk1)) - jnp.exp(jnp.sum(lq2 * lk2)) + lambda_init
        qdc, kdc, vdc = _da_inputs(p_ctx)
        qdl, kdl, vdl = _da_inputs(p_lat)
        qdl = _apply_rope(qdl, cos, sin)
        kdl = _apply_rope(kdl, cos, sin)
        k_all = jnp.concatenate([kdl, kdc], axis=1)
        v_all = jnp.concatenate([vdl, vdc], axis=1)
        od_l = _diff_attention(qdl, k_all, v_all, lam)

        mix_l = _merge_heads(hml_l, ol, od_l, ml_norm_w[l], da_norm_w[l], lambda_init, w_out[l])
        h1 = _layer_norm(DEEPNORM_ALPHA * h + g1 * mix_l, ln1_w[l], ln1_b[l])
        y = _expert_choice_ffn(_modulate(h1, sh2, sc2), w_router[l], w_gate[l], w_up[l], w_down[l])
        h_next = _layer_norm(DEEPNORM_ALPHA * h1 + g2 * y, ln2_w[l], ln2_b[l])

        if not last:
            od_c = _diff_attention(qdc, kdc, vdc, lam)
            mix_c = _merge_heads(hml_c, oc, od_c, ml_norm_w[l], da_norm_w[l], lambda_init, w_out[l])
            hc1 = _layer_norm(DEEPNORM_ALPHA * h_ctx + cg1 * mix_c, ln1_w[l], ln1_b[l])
            yc = _expert_choice_ffn(_modulate(hc1, csh2, csc2), w_router[l], w_gate[l], w_up[l], w_down[l])
            h_ctx = _layer_norm(DEEPNORM_ALPHA * hc1 + cg2 * yc, ln2_w[l], ln2_b[l])
        h = h_next
    return h
```

```python
import functools
import math

import jax
import jax.numpy as jnp
from jax import lax
from jax.experimental import pallas as pl
from jax.experimental.pallas import tpu as pltpu

F32 = jnp.float32
BF16 = jnp.bfloat16
I32 = jnp.int32

LANES = 128
SUBLANES = 8
GRID_W = 64
N_HEADS = 8
ML_DQK = 64
ML_DV = 128
ML_CONV_W = 5
ML_CHUNK = 128
DA_DQK = 64
DA_DV = 128
SEC = 1024
ROPE_BASE = 10000.0
LN_EPS = 1e-5
DEPTH = 1
DEEPNORM_ALPHA = (2.0 * DEPTH) ** 0.25
EC_CAPACITY_FACTOR = 2
VMEM_LIMIT = 56 * 1024 * 1024


def _sigmoid(x):
    return 1.0 / (1.0 + jnp.exp(-x))


def _log_sigmoid(x):
    return jnp.minimum(x, 0.0) - jnp.log(1.0 + jnp.exp(-jnp.abs(x)))


def _ln_rows(x):
    mu = jnp.mean(x, axis=-1, keepdims=True)
    xc = x - mu
    var = jnp.mean(xc * xc, axis=-1, keepdims=True)
    return xc * lax.rsqrt(var + LN_EPS)


def _split3(x):
    hi = x.astype(BF16)
    r1 = x - hi.astype(F32)
    mid = r1.astype(BF16)
    lo = (r1 - mid.astype(F32)).astype(BF16)
    return hi, mid, lo


def _dot(a, b):
    return jnp.dot(a, b, preferred_element_type=F32)


def _dot_exact_lhs(a_bf16, x_f32):
    hi, mid, lo = _split3(x_f32)
    return _dot(a_bf16, hi) + _dot(a_bf16, mid) + _dot(a_bf16, lo)


def _dot_exact_rhs(x_f32, a_bf16):
    hi, mid, lo = _split3(x_f32)
    return _dot(hi, a_bf16) + _dot(mid, a_bf16) + _dot(lo, a_bf16)


def _ada_body(c_ref, w_ref, b_ref, o_ref):
    c = c_ref[...]
    a = c * _sigmoid(c)
    hi, mid, lo = _split3(a)
    w = w_ref[...]
    whi, wmid, wlo = _split3(w)
    acc = _dot(hi, whi) + _dot(hi, wmid) + _dot(mid, whi)
    acc = acc + _dot(mid, wmid) + _dot(hi, wlo) + _dot(lo, whi)
    o_ref[...] = acc + b_ref[...]


def _ada(cin, w, b):
    rows, d = cin.shape
    n = w.shape[1]
    tn = _pick(n, (1024, 512, 256, 128))
    return pl.pallas_call(
        _ada_body,
        grid=(n // tn,),
        in_specs=[pl.BlockSpec((rows, d), lambda j: (0, 0)),
                  pl.BlockSpec((d, tn), lambda j: (0, j)),
                  pl.BlockSpec((1, tn), lambda j: (0, j))],
        out_specs=pl.BlockSpec((rows, tn), lambda j: (0, j)),
        out_shape=jax.ShapeDtypeStruct((rows, n), F32),
        compiler_params=pltpu.CompilerParams(dimension_semantics=("arbitrary",),
                                             vmem_limit_bytes=VMEM_LIMIT),
        name="ada_mod",
    )(cin, w, b.reshape(1, n))


def _rope_tile(acc, cosf, sins):
    outs = []
    lane = lax.broadcasted_iota(I32, (acc.shape[0], LANES), 1)
    first = (lane & (DA_DQK - 1)) < (DA_DQK // 2)
    for t in range(acc.shape[1] // LANES):
        a = acc[:, t * LANES:(t + 1) * LANES]
        partner = jnp.where(first, pltpu.roll(a, LANES - DA_DQK // 2, 1), pltpu.roll(a, DA_DQK // 2, 1))
        outs.append(a * cosf + partner * sins)
    return jnp.concatenate(outs, axis=1)


def _proj_body(*refs, rope):
    if rope:
        (x_ref, sh_ref, sc_ref, w_ref, wg_ref, bg_ref, cos_ref, sin_ref,
         pml_ref, pda_ref, g_ref, xn_ref) = refs
    else:
        (x_ref, sh_ref, sc_ref, w_ref, wg_ref, bg_ref, pml_ref, pda_ref, g_ref, xn_ref) = refs
    j = pl.program_id(2)

    @pl.when(j == 0)
    def _():
        y = _ln_rows(x_ref[0]) * (1.0 + sc_ref[0]) + sh_ref[0]
        yb = y.astype(BF16)
        xn_ref[...] = yb
        g_ref[0] = _dot(yb, wg_ref[...]) + bg_ref[...]

    acc = _dot(xn_ref[...], w_ref[...])

    @pl.when(j < 3)
    def _():
        pml_ref[0] = acc

    @pl.when(j == 3)
    def _():
        q = _rope_tile(acc, cos_ref[...], sin_ref[...]) if rope else acc
        pda_ref[0] = (q * (DA_DQK ** -0.5)).astype(BF16)

    @pl.when(j == 4)
    def _():
        k = _rope_tile(acc, cos_ref[...], sin_ref[...]) if rope else acc
        pda_ref[0] = k.astype(BF16)

    @pl.when(j == 5)
    def _():
        pda_ref[0] = acc.astype(BF16)


def _proj(x, sh, sc, w_main, w_g, b_g, rope_tabs, tm):
    bsz, n, d = x.shape
    per_batch = sh.shape[0] == bsz
    rope = rope_tabs is not None
    mod_map = (lambda b, i, j: (b, 0, 0)) if per_batch else (lambda b, i, j: (0, 0, 0))
    in_specs = [pl.BlockSpec((1, tm, d), lambda b, i, j: (b, i, 0)),
                pl.BlockSpec((1, 1, d), mod_map),
                pl.BlockSpec((1, 1, d), mod_map),
                pl.BlockSpec((d, SEC), lambda b, i, j: (0, j)),
                pl.BlockSpec((d, LANES), lambda b, i, j: (0, 0)),
                pl.BlockSpec((1, LANES), lambda b, i, j: (0, 0))]
    args = [x, sh, sc, w_main, w_g, b_g]
    if rope:
        in_specs += [pl.BlockSpec((tm, LANES), lambda b, i, j: (i, 0)),
                     pl.BlockSpec((tm, LANES), lambda b, i, j: (i, 0))]
        args += list(rope_tabs)
    return pl.pallas_call(
        functools.partial(_proj_body, rope=rope),
        grid=(bsz, n // tm, 6),
        in_specs=in_specs,
        out_specs=[pl.BlockSpec((1, tm, SEC), lambda b, i, j: (b, i, jnp.minimum(j, 2))),
                   pl.BlockSpec((1, tm, SEC), lambda b, i, j: (b, i, jnp.maximum(j - 3, 0))),
                   pl.BlockSpec((1, tm, LANES), lambda b, i, j: (b, i, 0))],
        out_shape=[jax.ShapeDtypeStruct((bsz, n, 3 * SEC), F32),
                   jax.ShapeDtypeStruct((bsz, n, 3 * SEC), BF16),
                   jax.ShapeDtypeStruct((bsz, n, LANES), F32)],
        scratch_shapes=[pltpu.VMEM((tm, d), BF16)],
        compiler_params=pltpu.CompilerParams(dimension_semantics=("parallel", "parallel", "arbitrary"),
                                             vmem_limit_bytes=VMEM_LIMIT),
        name="in_proj_rope" if rope else "in_proj",
    )(*args)


def _conv_body(prev_ref, cur_ref, next_ref, w_ref, b_ref, s_ref, o_ref):
    i = pl.program_id(1)
    last = pl.num_programs(1) - 1
    cur = cur_ref[0]
    t = cur.shape[0]
    prev = jnp.where(i > 0, prev_ref[0], 0.0)
    nxt = jnp.where(i < last, next_ref[0], 0.0)
    ext = jnp.concatenate([prev, cur, nxt], axis=0)
    acc = jnp.zeros_like(cur) + b_ref[...]
    half = ML_CONV_W // 2
    for k in range(ML_CONV_W):
        off = SUBLANES - half + k
        acc = acc + ext[off:off + t] * w_ref[k:k + 1, :]
    y = acc * _sigmoid(acc)
    o_ref[0] = (y * s_ref[...]).astype(BF16)


def _conv(p_ml, conv_w, conv_b, tt):
    bsz, n, _ = p_ml.shape
    nb8 = n // SUBLANES
    r = tt // SUBLANES
    scale = jnp.concatenate([jnp.full((1, SEC // 2), ML_DQK ** -0.5, F32), jnp.ones((1, SEC // 2), F32)], axis=1)
    return pl.pallas_call(
        _conv_body,
        grid=(bsz, n // tt),
        in_specs=[pl.BlockSpec((1, SUBLANES, SEC), lambda b, i: (b, jnp.maximum(i * r - 1, 0), 0)),
                  pl.BlockSpec((1, tt, SEC), lambda b, i: (b, i, 0)),
                  pl.BlockSpec((1, SUBLANES, SEC), lambda b, i: (b, jnp.minimum((i + 1) * r, nb8 - 1), 0)),
                  pl.BlockSpec((ML_CONV_W, SEC), lambda b, i: (0, 0)),
                  pl.BlockSpec((1, SEC), lambda b, i: (0, 0)),
                  pl.BlockSpec((1, SEC), lambda b, i: (0, 0))],
        out_specs=pl.BlockSpec((1, tt, SEC), lambda b, i: (b, i, 0)),
        out_shape=jax.ShapeDtypeStruct((bsz, n, SEC), BF16),
        compiler_params=pltpu.CompilerParams(dimension_semantics=("parallel", "parallel"),
                                             vmem_limit_bytes=VMEM_LIMIT),
        name="ml_conv",
    )(p_ml, p_ml, p_ml, conv_w, conv_b.reshape(1, SEC), scale)


def _mlstm_head(q, k, v, i_col, b_col, i_row, b_row, b_tot, mask, c_prev, n_prev, m_prev):
    dmat = jnp.where(mask, b_col - b_row + i_row, -jnp.inf)
    m_inter = b_col + m_prev
    m_t = jnp.maximum(m_inter, jnp.max(dmat, axis=-1, keepdims=True))
    w_inter = jnp.exp(m_inter - m_t)
    s = lax.dot_general(q, k, (((1,), (1,)), ((), ())), preferred_element_type=F32)
    w = jnp.exp(dmat - m_t) * s
    num = w_inter * _dot(q, c_prev.astype(BF16)) + _dot(w.astype(BF16), v)
    qn = jnp.sum(q.astype(F32) * n_prev, axis=-1, keepdims=True)
    den = w_inter * qn + jnp.sum(w, axis=-1, keepdims=True)
    h = num / jnp.maximum(jnp.abs(den), jnp.exp(-m_t))
    gdec_col = b_tot - b_col + i_col
    gdec_row = b_tot - b_row + i_row
    m_new = jnp.maximum(b_tot + m_prev, jnp.max(gdec_row, axis=-1, keepdims=True))
    a = jnp.exp(b_tot + m_prev - m_new)
    kw = k.astype(F32) * jnp.exp(gdec_col - m_new)
    c_new = a * c_prev + lax.dot_general(kw.astype(BF16), v, (((0,), (0,)), ((), ())),
                                         preferred_element_type=F32)
    n_new = a * n_prev + jnp.sum(kw, axis=0, keepdims=True)
    return h, c_new, n_new, m_new


def _mlstm_body(qf_ref, kf_ref, vf_ref, gf_ref, gtf_ref, qb_ref, kb_ref, vb_ref, gb_ref, gtb_ref,
                c0_ref, n0_ref, m0_ref, hf_ref, hb_ref, c_ref, n_ref, m_ref):
    @pl.when(pl.program_id(1) == 0)
    def _():
        c_ref[...] = c0_ref[...]
        n_ref[...] = n0_ref[...]
        m_ref[...] = m0_ref[...]

    L = ML_CHUNK
    r = lax.broadcasted_iota(I32, (L, L), 0)
    s = lax.broadcasted_iota(I32, (L, L), 1)
    mle = r <= s
    mge = r >= s
    mle_b = mle.astype(BF16)
    mge_b = mge.astype(BF16)
    dirs = ((qf_ref, kf_ref, vf_ref, gf_ref, gtf_ref, hf_ref), (qb_ref, kb_ref, vb_ref, gb_ref, gtb_ref, hb_ref))
    for d, (q_ref, k_ref, v_ref, g_ref, gt_ref, h_ref) in enumerate(dirs):
        g = g_ref[0]
        gt = gt_ref[0]
        bc = _dot_exact_lhs(mge_b if d == 0 else mle_b, _log_sigmoid(g))
        br = _dot_exact_rhs(_log_sigmoid(gt), mle_b if d == 0 else mge_b)
        mask = mge if d == 0 else mle
        qa = q_ref[0]
        ka = k_ref[0]
        va = v_ref[0].astype(BF16)
        for h in range(N_HEADS):
            ci = 2 * N_HEADS * d + h
            cf = ci + N_HEADS
            b_row = br[cf:cf + 1, :]
            b_tot = b_row[:, L - 1:L] if d == 0 else b_row[:, 0:1]
            hh, c_new, n_new, m_new = _mlstm_head(
                qa[:, h * ML_DQK:(h + 1) * ML_DQK], ka[:, h * ML_DQK:(h + 1) * ML_DQK],
                va[:, h * ML_DV:(h + 1) * ML_DV],
                g[:, ci:ci + 1], bc[:, cf:cf + 1], gt[ci:ci + 1, :], b_row, b_tot, mask,
                c_ref[0, d, h], n_ref[0, d, h], m_ref[0, d, h][:, 0:1])
            h_ref[0, :, h * ML_DV:(h + 1) * ML_DV] = hh
            c_ref[0, d, h] = c_new
            n_ref[0, d, h] = n_new
            m_ref[0, d, h] = jnp.broadcast_to(m_new, (1, LANES))


def _mlstm(qkc, p_ml, gates, gates_t, c0, n0, m0):
    bsz, n, _ = qkc.shape
    L = ML_CHUNK
    nc = n // L
    hq = SEC // 2
    fwd = lambda b, i: (b, i, 0)
    bwd = lambda b, i: (b, nc - 1 - i, 0)
    st = lambda b, i: (b, 0, 0, 0, 0)
    c_spec = pl.BlockSpec((1, 2, N_HEADS, ML_DQK, ML_DV), st)
    n_spec = pl.BlockSpec((1, 2, N_HEADS, 1, ML_DQK), st)
    m_spec = pl.BlockSpec((1, 2, N_HEADS, 1, LANES), st)

    def dir_specs(pos):
        return [pl.BlockSpec((1, L, hq), lambda b, i: (b, pos(i), 0)),
                pl.BlockSpec((1, L, hq), lambda b, i: (b, pos(i), 1)),
                pl.BlockSpec((1, L, SEC), lambda b, i: (b, pos(i), 1)),
                pl.BlockSpec((1, L, LANES), lambda b, i: (b, pos(i), 0)),
                pl.BlockSpec((1, 4 * N_HEADS, L), lambda b, i: (b, 0, pos(i)))]

    return pl.pallas_call(
        _mlstm_body,
        grid=(bsz, nc),
        in_specs=dir_specs(lambda i: i) + dir_specs(lambda i: nc - 1 - i) + [c_spec, n_spec, m_spec],
        out_specs=[pl.BlockSpec((1, L, SEC), fwd), pl.BlockSpec((1, L, SEC), bwd), c_spec, n_spec, m_spec],
        out_shape=[jax.ShapeDtypeStruct((bsz, n, SEC), F32), jax.ShapeDtypeStruct((bsz, n, SEC), F32),
                   jax.ShapeDtypeStruct(c0.shape, F32), jax.ShapeDtypeStruct(n0.shape, F32),
                   jax.ShapeDtypeStruct(m0.shape, F32)],
        compiler_params=pltpu.CompilerParams(dimension_semantics=("parallel", "arbitrary"),
                                             vmem_limit_bytes=VMEM_LIMIT),
        name="mlstm",
    )(qkc, qkc, p_ml, gates, gates_t, qkc, qkc, p_ml, gates, gates_t, c0, n0, m0)


def _attn_body(lam_ref, q_ref, kt_ref, v_ref, o_ref, m_s, l_s, acc_s, *, tk, lambda_init):
    q = q_ref[0]
    tq = q.shape[0]
    lane = lax.broadcasted_iota(I32, q.shape, 1)
    zero = jnp.zeros_like(q)
    qs = (jnp.where(lane < DA_DQK, q, zero), jnp.where(lane >= DA_DQK, q, zero))
    m_s[...] = jnp.full(m_s.shape, -jnp.inf, F32)
    l_s[...] = jnp.zeros(l_s.shape, F32)
    acc_s[...] = jnp.zeros(acc_s.shape, F32)
    nk = kt_ref.shape[3] // tk

    def body(t, carry):
        off = pl.multiple_of(t * tk, LANES)
        ks = kt_ref[0, 0, :, pl.ds(off, tk)]
        vs = v_ref[0, 0, pl.ds(off, tk), :]
        for c in range(2):
            s = _dot(qs[c], ks)
            m_old = m_s[c]
            m_new = jnp.maximum(m_old, jnp.max(s, axis=-1, keepdims=True))
            p = jnp.exp(s - m_new)
            alpha = jnp.exp(m_old - m_new)
            l_s[c] = alpha * l_s[c] + jnp.sum(p, axis=-1, keepdims=True)
            acc_s[c] = alpha * acc_s[c] + _dot(p.astype(BF16), vs)
            m_s[c] = m_new
        return carry

    lax.fori_loop(0, nk, body, 0)
    lm = lam_ref[...]
    lam = (jnp.exp(jnp.sum(lm[0:1] * lm[1:2], axis=-1, keepdims=True))
           - jnp.exp(jnp.sum(lm[2:3] * lm[3:4], axis=-1, keepdims=True)) + lambda_init)
    o_ref[0] = acc_s[0] / l_s[0] - lam * (acc_s[1] / l_s[1])


def _attn(da_lambda, p_da, kt, vh, tq, tk, lambda_init):
    bsz, n, _ = p_da.shape
    m = kt.shape[3]
    hd = 2 * DA_DQK
    return pl.pallas_call(
        functools.partial(_attn_body, tk=tk, lambda_init=lambda_init),
        grid=(bsz, N_HEADS, n // tq),
        in_specs=[pl.BlockSpec((4, DA_DQK), lambda b, h, i: (0, 0)),
                  pl.BlockSpec((1, tq, hd), lambda b, h, i: (b, i, h)),
                  pl.BlockSpec((1, 1, hd, m), lambda b, h, i: (b, h, 0, 0)),
                  pl.BlockSpec((1, 1, m, DA_DV), lambda b, h, i: (b, h, 0, 0))],
        out_specs=pl.BlockSpec((1, tq, DA_DV), lambda b, h, i: (b, i, h)),
        out_shape=jax.ShapeDtypeStruct((bsz, n, SEC), F32),
        scratch_shapes=[pltpu.VMEM((2, tq, 1), F32), pltpu.VMEM((2, tq, 1), F32),
                        pltpu.VMEM((2, tq, DA_DV), F32)],
        compiler_params=pltpu.CompilerParams(dimension_semantics=("parallel", "parallel", "arbitrary"),
                                             vmem_limit_bytes=VMEM_LIMIT),
        name="diff_attn",
    )(da_lambda, p_da, kt, vh)


def _merge_body(hf_ref, hb_ref, o_ref, od_ref, x_ref, wout_ref, mlw_ref, daw_ref, g1_ref, ln1w_ref, ln1b_ref,
                sh2_ref, sc2_ref, wr_ref, h1_ref, u_ref, aff_ref, *, lambda_init, n_experts):
    hsum = hf_ref[0] + hb_ref[0]
    og = o_ref[0]
    od = od_ref[0]
    mlw = mlw_ref[...]
    daw = daw_ref[...]
    parts = []
    for h in range(N_HEADS):
        sl = slice(h * ML_DV, (h + 1) * ML_DV)
        parts.append(_ln_rows(hsum[:, sl]) * mlw[:, sl] * _sigmoid(og[:, sl]))
    for h in range(N_HEADS):
        sl = slice(h * DA_DV, (h + 1) * DA_DV)
        z = od[:, sl]
        zn = z * lax.rsqrt(jnp.mean(z * z, axis=-1, keepdims=True) + LN_EPS)
        parts.append(zn * daw[:, sl] * (1.0 - lambda_init))
    ycat = jnp.concatenate(parts, axis=1).astype(BF16)
    mix = _dot(ycat, wout_ref[...])
    h1 = _ln_rows(DEEPNORM_ALPHA * x_ref[0] + g1_ref[0] * mix) * ln1w_ref[...] + ln1b_ref[...]
    h1_ref[0] = h1
    u = _ln_rows(h1) * (1.0 + sc2_ref[0]) + sh2_ref[0]
    u_ref[0] = u
    uhi, umid, _ = _split3(u)
    whi, wmid, _ = _split3(wr_ref[...])
    logits = _dot(uhi, whi) + _dot(uhi, wmid) + _dot(umid, whi)
    lane = lax.broadcasted_iota(I32, logits.shape, 1)
    logits = jnp.where(lane < n_experts, logits, -jnp.inf)
    e = jnp.exp(logits - jnp.max(logits, axis=-1, keepdims=True))
    aff_ref[0] = e / jnp.sum(e, axis=-1, keepdims=True)


def _merge(hf, hb, p_ml, od, x, w_out, ml_norm_w, da_norm_w, g1, ln1_w, ln1_b, sh2, sc2, w_router_pad,
           tm, lambda_init, n_experts):
    bsz, n, d = x.shape
    row = lambda b, i: (b, i, 0)
    vec = lambda b, i: (0, 0)
    mod = lambda b, i: (b, 0, 0)
    return pl.pallas_call(
        functools.partial(_merge_body, lambda_init=lambda_init, n_experts=n_experts),
        grid=(bsz, n // tm),
        in_specs=[pl.BlockSpec((1, tm, SEC), row), pl.BlockSpec((1, tm, SEC), row),
                  pl.BlockSpec((1, tm, SEC), lambda b, i: (b, i, 2)),
                  pl.BlockSpec((1, tm, SEC), row),
                  pl.BlockSpec((1, tm, d), row),
                  pl.BlockSpec((2 * SEC, d), vec),
                  pl.BlockSpec((1, SEC), vec), pl.BlockSpec((1, SEC), vec),
                  pl.BlockSpec((1, 1, d), mod),
                  pl.BlockSpec((1, d), vec), pl.BlockSpec((1, d), vec),
                  pl.BlockSpec((1, 1, d), mod), pl.BlockSpec((1, 1, d), mod),
                  pl.BlockSpec((d, LANES), vec)],
        out_specs=[pl.BlockSpec((1, tm, d), row), pl.BlockSpec((1, tm, d), row),
                   pl.BlockSpec((1, tm, LANES), row)],
        out_shape=[jax.ShapeDtypeStruct((bsz, n, d), F32), jax.ShapeDtypeStruct((bsz, n, d), F32),
                   jax.ShapeDtypeStruct((bsz, n, LANES), F32)],
        compiler_params=pltpu.CompilerParams(dimension_semantics=("parallel", "parallel"),
                                             vmem_limit_bytes=VMEM_LIMIT),
        name="merge_ln1_router",
    )(hf, hb, p_ml, od, x, w_out, ml_norm_w.reshape(1, SEC), da_norm_w.reshape(1, SEC), g1,
      ln1_w.reshape(1, d), ln1_b.reshape(1, d), sh2, sc2, w_router_pad)


def _cumsum_tokens(maskf, uincl_b, lstrict_b):
    mb = maskf.astype(BF16)
    win = _dot(mb, uincl_b)
    tot = jnp.broadcast_to(win[:, LANES - 1:LANES], win.shape)
    offs = _dot(lstrict_b, tot.astype(BF16))
    return offs + win, tot


def _router_body(a_ref, idx_ref, g_ref, *, cap):
    a = a_ref[0]
    rr = a.shape[0]
    bits = lax.bitcast_convert_type(a, I32)
    cur = jnp.zeros((1, 1), I32)
    for bit in range(30, -1, -1):
        cand = cur | (1 << bit)
        cnt = jnp.sum((bits >= cand).astype(I32), keepdims=True)
        cur = jnp.where(cnt >= cap, cand, cur)
    gt = bits > cur
    eq = bits == cur
    need = (cap - jnp.sum(gt.astype(I32), keepdims=True)).astype(F32)

    li = lax.broadcasted_iota(I32, (LANES, LANES), 0)
    lj = lax.broadcasted_iota(I32, (LANES, LANES), 1)
    uincl_b = (li <= lj).astype(BF16)
    ri = lax.broadcasted_iota(I32, (rr, rr), 0)
    rj = lax.broadcasted_iota(I32, (rr, rr), 1)
    lstrict_b = (rj < ri).astype(BF16)
    rincl_b = (ri <= rj).astype(BF16)

    eqf = eq.astype(F32)
    eq_incl, _ = _cumsum_tokens(eqf, uincl_b, lstrict_b)
    sel = gt | (eq & ((eq_incl - eqf) < need))
    self_ = sel.astype(F32)
    csum, _ = _cumsum_tokens(self_, uincl_b, lstrict_b)

    ones8 = jnp.ones((SUBLANES, LANES), BF16)
    tot_lane = lax.dot_general(ones8, self_.astype(BF16), (((1,), (1,)), ((), ())),
                               preferred_element_type=F32)
    end_lane = _dot(tot_lane.astype(BF16), rincl_b)[0:1]
    beg_lane = end_lane - tot_lane[0:1]
    j = lax.broadcasted_iota(I32, (cap, 1), 0).astype(F32)
    onehot = ((beg_lane <= j) & (end_lane > j)).astype(BF16)
    row_idx = jnp.sum((end_lane <= j).astype(F32), axis=-1, keepdims=True)
    chi = jnp.floor(csum * (1.0 / LANES))
    clo = csum - chi * LANES
    crow = _dot(onehot, chi.astype(BF16)) * LANES + _dot(onehot, clo.astype(BF16))
    lane_idx = jnp.sum((crow <= j).astype(F32), axis=-1, keepdims=True)
    idx_ref[0] = (row_idx * LANES + lane_idx).astype(I32)
    ahi, amid, alo = _split3(a)
    arow = _dot(onehot, ahi) + _dot(onehot, amid) + _dot(onehot, alo)
    lane = lax.broadcasted_iota(I32, (cap, LANES), 1).astype(F32)
    g_ref[0] = jnp.sum(jnp.where(lane == lane_idx, arow, 0.0), axis=-1, keepdims=True)


def _router(aff_t, cap):
    g, rr, _ = aff_t.shape
    return pl.pallas_call(
        functools.partial(_router_body, cap=cap),
        grid=(g,),
        in_specs=[pl.BlockSpec((1, rr, LANES), lambda i: (i, 0, 0))],
        out_specs=[pl.BlockSpec((1, cap, 1), lambda i: (i, 0, 0)), pl.BlockSpec((1, cap, 1), lambda i: (i, 0, 0))],
        out_shape=[jax.ShapeDtypeStruct((g, cap, 1), I32), jax.ShapeDtypeStruct((g, cap, 1), F32)],
        compiler_params=pltpu.CompilerParams(dimension_semantics=("parallel",), vmem_limit_bytes=VMEM_LIMIT),
        name="ec_select",
    )(aff_t)


def _moe_body(idx_ref, g_ref, u_hbm, wg_ref, wu_ref, wd_ref, y_in_hbm, y_hbm, buf, xs, tmp, sem,
              *, cap, n_tok, n_experts, rc):
    del y_in_hbm
    grp = pl.program_id(0)
    h = pl.program_id(1)
    base = (grp // n_experts) * n_tok

    @pl.when(h == 0)
    def _():
        def issue(j, carry):
            row = base + idx_ref[0, 0, j]
            pltpu.make_async_copy(u_hbm.at[pl.ds(row, 1)], buf.at[pl.ds(j, 1)], sem.at[0]).start()
            return carry
        lax.fori_loop(0, cap, issue, 0)
        pltpu.make_async_copy(u_hbm.at[pl.ds(0, cap)], buf, sem.at[0]).wait()
        xs[...] = buf[...].astype(BF16)
        buf[...] = jnp.zeros(buf.shape, F32)

    x = xs[...]
    a = _dot(x, wg_ref[0].astype(BF16))
    b = _dot(x, wu_ref[0].astype(BF16))
    hid = (a * _sigmoid(a) * b).astype(BF16)
    buf[...] += _dot(hid, wd_ref[0].astype(BF16))

    @pl.when(h == pl.num_programs(1) - 1)
    def _():
        for c in range(cap // rc):
            def gather(r, carry, c=c):
                row = base + idx_ref[0, 0, c * rc + r]
                pltpu.make_async_copy(y_hbm.at[pl.ds(row, 1)], tmp.at[pl.ds(r, 1)], sem.at[1]).start()
                return carry
            lax.fori_loop(0, rc, gather, 0)
            pltpu.make_async_copy(y_hbm.at[pl.ds(0, rc)], tmp, sem.at[1]).wait()
            tmp[...] = tmp[...] + g_ref[0, c * rc:(c + 1) * rc, :] * buf[c * rc:(c + 1) * rc, :]

            def scatter(r, carry, c=c):
                row = base + idx_ref[0, 0, c * rc + r]
                pltpu.make_async_copy(tmp.at[pl.ds(r, 1)], y_hbm.at[pl.ds(row, 1)], sem.at[2]).start()
                return carry
            lax.fori_loop(0, rc, scatter, 0)
            pltpu.make_async_copy(tmp, y_hbm.at[pl.ds(0, rc)], sem.at[2]).wait()


def _moe(idx3, gsel, u2, w_gate, w_up, w_down, y0, n_tok, th):
    g, _, cap = idx3.shape
    n_experts, d, de = w_gate.shape
    rc = min(256, cap)
    return pl.pallas_call(
        functools.partial(_moe_body, cap=cap, n_tok=n_tok, n_experts=n_experts, rc=rc),
        grid=(g, de // th),
        in_specs=[pl.BlockSpec((1, 1, cap), lambda i, h: (i, 0, 0), memory_space=pltpu.SMEM),
                  pl.BlockSpec((1, cap, 1), lambda i, h: (i, 0, 0)),
                  pl.BlockSpec(memory_space=pl.ANY),
                  pl.BlockSpec((1, d, th), lambda i, h: (i % n_experts, 0, h)),
                  pl.BlockSpec((1, d, th), lambda i, h: (i % n_experts, 0, h)),
                  pl.BlockSpec((1, th, d), lambda i, h: (i % n_experts, h, 0)),
                  pl.BlockSpec(memory_space=pl.ANY)],
        out_specs=pl.BlockSpec(memory_space=pl.ANY),
        out_shape=jax.ShapeDtypeStruct(y0.shape, F32),
        scratch_shapes=[pltpu.VMEM((cap, d), F32), pltpu.VMEM((cap, d), BF16), pltpu.VMEM((rc, d), F32),
                        pltpu.SemaphoreType.DMA((3,))],
        input_output_aliases={6: 0},
        compiler_params=pltpu.CompilerParams(dimension_semantics=("arbitrary", "arbitrary"),
                                             vmem_limit_bytes=VMEM_LIMIT),
        name="ec_moe",
    )(idx3, gsel, u2, w_gate, w_up, w_down, y0)


def _final_body(h1_ref, y_ref, g2_ref, w_ref, b_ref, o_ref):
    t = DEEPNORM_ALPHA * h1_ref[0] + g2_ref[0] * y_ref[0]
    o_ref[0] = _ln_rows(t) * w_ref[...] + b_ref[...]


def _final(h1, y, g2, ln2_w, ln2_b, tm):
    bsz, n, d = h1.shape
    row = lambda b, i: (b, i, 0)
    return pl.pallas_call(
        _final_body,
        grid=(bsz, n // tm),
        in_specs=[pl.BlockSpec((1, tm, d), row), pl.BlockSpec((1, tm, d), row),
                  pl.BlockSpec((1, 1, d), lambda b, i: (b, 0, 0)),
                  pl.BlockSpec((1, d), lambda b, i: (0, 0)), pl.BlockSpec((1, d), lambda b, i: (0, 0))],
        out_specs=pl.BlockSpec((1, tm, d), row),
        out_shape=jax.ShapeDtypeStruct((bsz, n, d), F32),
        compiler_params=pltpu.CompilerParams(dimension_semantics=("parallel", "parallel"),
                                             vmem_limit_bytes=VMEM_LIMIT),
        name="ln2",
    )(h1, y, g2, ln2_w.reshape(1, d), ln2_b.reshape(1, d))


def _rope_tables(n):
    rows = n // GRID_W
    pos_row = jnp.repeat(jnp.arange(rows, dtype=I32), GRID_W).astype(F32)
    pos_col = jnp.tile(jnp.arange(GRID_W, dtype=I32), rows).astype(F32)
    n_freq = DA_DQK // 4
    inv = ROPE_BASE ** (-jnp.arange(n_freq, dtype=F32) / n_freq)
    ang = jnp.concatenate([pos_row[:, None] * inv, pos_col[:, None] * inv], -1)
    cos, sin = jnp.cos(ang), jnp.sin(ang)
    reps = LANES // DA_DQK
    cosf = jnp.tile(cos, (1, 2 * reps))
    sins = jnp.tile(jnp.concatenate([-sin, sin], -1), (1, reps))
    return cosf, sins


def _pick(n, prefs):
    for t in prefs:
        if n % t == 0:
            return t
    return n


def kernel(x, c, ctx, c_ctx, w_ada, b_ada, w_in, b_gates, ml_conv_w, ml_conv_b, ml_norm_w, da_lambda, da_norm_w,
           w_out, ln1_w, ln1_b, w_router, w_gate, w_up, w_down, ln2_w, ln2_b):
    bsz, n, d = x.shape
    n_ctx = ctx.shape[1]
    n_experts = w_gate.shape[1]
    cap = EC_CAPACITY_FACTOR * n // n_experts
    l = 0
    lambda_init = 0.8 - 0.6 * math.exp(-0.3 * l)

    cin = jnp.zeros((SUBLANES, d), F32).at[:bsz].set(c).at[bsz].set(c_ctx)
    mod = _ada(cin, w_ada[l], b_ada[l])
    sh1, sc1, g1, sh2, sc2, g2 = (mod[:bsz, k * d:(k + 1) * d][:, None, :] for k in range(6))
    csh1, csc1 = (mod[bsz:bsz + 1, k * d:(k + 1) * d][:, None, :] for k in range(2))

    w = w_in[l]
    off_gates = 3 * SEC
    n_gate = 4 * N_HEADS
    w_main = jnp.concatenate([w[:, :off_gates], w[:, off_gates + n_gate:]], axis=1).astype(BF16)
    w_g = jnp.pad(w[:, off_gates:off_gates + n_gate], ((0, 0), (0, LANES - n_gate))).astype(BF16)
    b_g = jnp.pad(b_gates[l], (0, LANES - n_gate)).reshape(1, LANES)
    pml_l, pda_l, gates_l = _proj(x, sh1, sc1, w_main, w_g, b_g, _rope_tables(n), _pick(n, (512, 256, 128)))
    pml_c, pda_c, gates_c = _proj(ctx, csh1, csc1, w_main, w_g, b_g, None, _pick(n_ctx, (512, 256, 128)))

    qkc_l = _conv(pml_l, ml_conv_w[l], ml_conv_b[l], _pick(n, (512, 256, 128)))
    qkc_c = _conv(pml_c, ml_conv_w[l], ml_conv_b[l], _pick(n_ctx, (512, 256, 128)))
    gt_l = jnp.swapaxes(gates_l[:, :, :n_gate], 1, 2)
    gt_c = jnp.swapaxes(gates_c[:, :, :n_gate], 1, 2)
    c0 = jnp.zeros((bsz, 2, N_HEADS, ML_DQK, ML_DV), F32)
    n0 = jnp.zeros((bsz, 2, N_HEADS, 1, ML_DQK), F32)
    m0 = jnp.zeros((bsz, 2, N_HEADS, 1, LANES), F32)
    _, _, c1, n1, m1 = _mlstm(qkc_c, pml_c, gates_c, gt_c, c0, n0, m0)
    hf, hb, _, _, _ = _mlstm(qkc_l, pml_l, gates_l, gt_l, c1, n1, m1)

    m_all = n + n_ctx
    k_all = jnp.concatenate([pda_l[:, :, SEC:2 * SEC], pda_c[:, :, SEC:2 * SEC]], axis=1)
    v_all = jnp.concatenate([pda_l[:, :, 2 * SEC:], pda_c[:, :, 2 * SEC:]], axis=1)
    kt = k_all.reshape(bsz, m_all, N_HEADS, 2 * DA_DQK).transpose(0, 2, 3, 1)
    vh = v_all.reshape(bsz, m_all, N_HEADS, DA_DV).transpose(0, 2, 1, 3)
    od = _attn(da_lambda[l], pda_l, kt, vh, _pick(n, (512, 256, 128)), _pick(m_all, (640, 256, 128)), lambda_init)

    w_router_pad = jnp.pad(w_router[l], ((0, 0), (0, LANES - n_experts)))
    h1, u, aff = _merge(hf, hb, pml_l, od, x, w_out[l].astype(BF16), ml_norm_w[l], da_norm_w[l], g1,
                        ln1_w[l], ln1_b[l], sh2, sc2, w_router_pad, _pick(n, (256, 128)), lambda_init, n_experts)

    aff_t = jnp.swapaxes(aff[:, :, :n_experts], 1, 2).reshape(bsz * n_experts, n // LANES, LANES)
    idx, gsel = _router(aff_t, cap)
    y = _moe(idx.reshape(bsz * n_experts, 1, cap), gsel, u.reshape(bsz * n, d), w_gate[l], w_up[l], w_down[l],
             jnp.zeros((bsz * n, d), F32), n, _pick(w_gate.shape[3], (256, 128)))
    return _final(h1, y.reshape(bsz, n, d), g2, ln2_w[l], ln2_b[l], _pick(n, (512, 256, 128)))
```

```python
import functools
import math

import jax
import jax.numpy as jnp
from jax import lax
from jax.experimental import pallas as pl
from jax.experimental.pallas import tpu as pltpu

F32 = jnp.float32
BF16 = jnp.bfloat16
I32 = jnp.int32

LANES = 128
SUBLANES = 8
GRID_W = 64
N_HEADS = 8
ML_DQK = 64
ML_DV = 128
ML_CONV_W = 5
ML_CHUNK = 128
DA_DQK = 64
DA_DV = 128
SEC = 1024
ROPE_BASE = 10000.0
LN_EPS = 1e-5
DEPTH = 1
DEEPNORM_ALPHA = (2.0 * DEPTH) ** 0.25
EC_CAPACITY_FACTOR = 2
VMEM_LIMIT = 56 * 1024 * 1024
DA_QSCALE = DA_DQK ** -0.5 * math.log2(math.e)
DMA_UNROLL = 8
DA_VROWS = DA_DV + 16


def _sigmoid(x):
    return 1.0 / (1.0 + jnp.exp(-x))


def _log_sigmoid(x):
    return jnp.minimum(x, 0.0) - jnp.log(1.0 + jnp.exp(-jnp.abs(x)))


def _ln_rows(x):
    mu = jnp.mean(x, axis=-1, keepdims=True)
    xc = x - mu
    var = jnp.mean(xc * xc, axis=-1, keepdims=True)
    return xc * lax.rsqrt(var + LN_EPS)


def _split3(x):
    hi = x.astype(BF16)
    r1 = x - hi.astype(F32)
    mid = r1.astype(BF16)
    lo = (r1 - mid.astype(F32)).astype(BF16)
    return hi, mid, lo


def _dot(a, b):
    return jnp.dot(a, b, preferred_element_type=F32)


def _dot_exact_lhs(a_bf16, x_f32):
    hi, mid, lo = _split3(x_f32)
    return _dot(a_bf16, hi) + _dot(a_bf16, mid) + _dot(a_bf16, lo)


def _dot_exact_rhs(x_f32, a_bf16):
    hi, mid, lo = _split3(x_f32)
    return _dot(hi, a_bf16) + _dot(mid, a_bf16) + _dot(lo, a_bf16)


def _ada_body(c_ref, w_ref, b_ref, o_ref):
    c = c_ref[...]
    a = c * _sigmoid(c)
    hi, mid, lo = _split3(a)
    w = w_ref[...]
    whi, wmid, wlo = _split3(w)
    acc = _dot(hi, whi) + _dot(hi, wmid) + _dot(mid, whi)
    acc = acc + _dot(mid, wmid) + _dot(hi, wlo) + _dot(lo, whi)
    o_ref[...] = acc + b_ref[...]


def _ada(cin, w, b):
    rows, d = cin.shape
    n = w.shape[1]
    tn = _pick(n, (1024, 512, 256, 128))
    return pl.pallas_call(
        _ada_body,
        grid=(n // tn,),
        in_specs=[pl.BlockSpec((rows, d), lambda j: (0, 0)),
                  pl.BlockSpec((d, tn), lambda j: (0, j)),
                  pl.BlockSpec((1, tn), lambda j: (0, j))],
        out_specs=pl.BlockSpec((rows, tn), lambda j: (0, j)),
        out_shape=jax.ShapeDtypeStruct((rows, n), F32),
        compiler_params=pltpu.CompilerParams(dimension_semantics=("arbitrary",),
                                             vmem_limit_bytes=VMEM_LIMIT),
        name="ada_mod",
    )(cin, w, b.reshape(1, n))


def _rope_tile(acc, cosf, sins):
    outs = []
    lane = lax.broadcasted_iota(I32, (acc.shape[0], LANES), 1)
    first = (lane & (DA_DQK - 1)) < (DA_DQK // 2)
    for t in range(acc.shape[1] // LANES):
        a = acc[:, t * LANES:(t + 1) * LANES]
        partner = jnp.where(first, pltpu.roll(a, LANES - DA_DQK // 2, 1), pltpu.roll(a, DA_DQK // 2, 1))
        outs.append(a * cosf + partner * sins)
    return jnp.concatenate(outs, axis=1)


def _proj_body(*refs, rope):
    if rope:
        (x_ref, sh_ref, sc_ref, w_ref, wg_ref, bg_ref, cos_ref, sin_ref,
         pml_ref, pda_ref, g_ref, xn_ref) = refs
    else:
        (x_ref, sh_ref, sc_ref, w_ref, wg_ref, bg_ref, pml_ref, pda_ref, g_ref, xn_ref) = refs
    j = pl.program_id(2)

    @pl.when(j == 0)
    def _():
        y = _ln_rows(x_ref[0]) * (1.0 + sc_ref[0]) + sh_ref[0]
        yb = y.astype(BF16)
        xn_ref[...] = yb
        g_ref[0] = _dot(yb, wg_ref[...]) + bg_ref[...]

    acc = _dot(xn_ref[...], w_ref[...])

    @pl.when(j < 3)
    def _():
        pml_ref[0] = acc

    @pl.when(j == 3)
    def _():
        q = _rope_tile(acc, cos_ref[...], sin_ref[...]) if rope else acc
        pda_ref[0] = (q * DA_QSCALE).astype(BF16)

    @pl.when(j == 4)
    def _():
        k = _rope_tile(acc, cos_ref[...], sin_ref[...]) if rope else acc
        pda_ref[0] = k.astype(BF16)

    @pl.when(j == 5)
    def _():
        pda_ref[0] = acc.astype(BF16)


def _proj(x, sh, sc, w_main, w_g, b_g, rope_tabs, tm):
    bsz, n, d = x.shape
    per_batch = sh.shape[0] == bsz
    rope = rope_tabs is not None
    mod_map = (lambda b, i, j: (b, 0, 0)) if per_batch else (lambda b, i, j: (0, 0, 0))
    in_specs = [pl.BlockSpec((1, tm, d), lambda b, i, j: (b, i, 0)),
                pl.BlockSpec((1, 1, d), mod_map),
                pl.BlockSpec((1, 1, d), mod_map),
                pl.BlockSpec((d, SEC), lambda b, i, j: (0, j)),
                pl.BlockSpec((d, LANES), lambda b, i, j: (0, 0)),
                pl.BlockSpec((1, LANES), lambda b, i, j: (0, 0))]
    args = [x, sh, sc, w_main, w_g, b_g]
    if rope:
        in_specs += [pl.BlockSpec((tm, LANES), lambda b, i, j: (i, 0)),
                     pl.BlockSpec((tm, LANES), lambda b, i, j: (i, 0))]
        args += list(rope_tabs)
    return pl.pallas_call(
        functools.partial(_proj_body, rope=rope),
        grid=(bsz, n // tm, 6),
        in_specs=in_specs,
        out_specs=[pl.BlockSpec((1, tm, SEC), lambda b, i, j: (b, i, jnp.minimum(j, 2))),
                   pl.BlockSpec((1, tm, SEC), lambda b, i, j: (b, i, jnp.maximum(j - 3, 0))),
                   pl.BlockSpec((1, tm, LANES), lambda b, i, j: (b, i, 0))],
        out_shape=[jax.ShapeDtypeStruct((bsz, n, 3 * SEC), F32),
                   jax.ShapeDtypeStruct((bsz, n, 3 * SEC), BF16),
                   jax.ShapeDtypeStruct((bsz, n, LANES), F32)],
        scratch_shapes=[pltpu.VMEM((tm, d), BF16)],
        compiler_params=pltpu.CompilerParams(dimension_semantics=("parallel", "parallel", "arbitrary"),
                                             vmem_limit_bytes=VMEM_LIMIT),
        name="in_proj_rope" if rope else "in_proj",
    )(*args)


def _conv_body(prev_ref, cur_ref, next_ref, w_ref, b_ref, s_ref, o_ref):
    i = pl.program_id(1)
    last = pl.num_programs(1) - 1
    cur = cur_ref[0]
    t = cur.shape[0]
    prev = jnp.where(i > 0, prev_ref[0], 0.0)
    nxt = jnp.where(i < last, next_ref[0], 0.0)
    ext = jnp.concatenate([prev, cur, nxt], axis=0)
    acc = jnp.zeros_like(cur) + b_ref[...]
    half = ML_CONV_W // 2
    for k in range(ML_CONV_W):
        off = SUBLANES - half + k
        acc = acc + ext[off:off + t] * w_ref[k:k + 1, :]
    y = acc * _sigmoid(acc)
    o_ref[0] = (y * s_ref[...]).astype(BF16)


def _conv(p_ml, conv_w, conv_b, tt):
    bsz, n, _ = p_ml.shape
    nb8 = n // SUBLANES
    r = tt // SUBLANES
    scale = jnp.concatenate([jnp.full((1, SEC // 2), ML_DQK ** -0.5, F32), jnp.ones((1, SEC // 2), F32)], axis=1)
    return pl.pallas_call(
        _conv_body,
        grid=(bsz, n // tt),
        in_specs=[pl.BlockSpec((1, SUBLANES, SEC), lambda b, i: (b, jnp.maximum(i * r - 1, 0), 0)),
                  pl.BlockSpec((1, tt, SEC), lambda b, i: (b, i, 0)),
                  pl.BlockSpec((1, SUBLANES, SEC), lambda b, i: (b, jnp.minimum((i + 1) * r, nb8 - 1), 0)),
                  pl.BlockSpec((ML_CONV_W, SEC), lambda b, i: (0, 0)),
                  pl.BlockSpec((1, SEC), lambda b, i: (0, 0)),
                  pl.BlockSpec((1, SEC), lambda b, i: (0, 0))],
        out_specs=pl.BlockSpec((1, tt, SEC), lambda b, i: (b, i, 0)),
        out_shape=jax.ShapeDtypeStruct((bsz, n, SEC), BF16),
        compiler_params=pltpu.CompilerParams(dimension_semantics=("parallel", "parallel"),
                                             vmem_limit_bytes=VMEM_LIMIT),
        name="ml_conv",
    )(p_ml, p_ml, p_ml, conv_w, conv_b.reshape(1, SEC), scale)


def _mlstm_head(q, kt, v_ext, b_col, i_row, b_row, b_tot, mask, c_prev, m_prev):
    dmat = jnp.where(mask, b_col + (i_row - b_row), -jnp.inf)
    m_inter = b_col + m_prev
    m_t = jnp.maximum(m_inter, jnp.max(dmat, axis=-1, keepdims=True))
    w_inter = jnp.exp(m_inter - m_t)
    w = jnp.exp(dmat - m_t) * _dot(q, kt)
    ext = w_inter * _dot(q, c_prev.astype(BF16)) + _dot(w.astype(BF16), v_ext)
    h = ext[:, :ML_DV] / jnp.maximum(jnp.abs(ext[:, ML_DV:ML_DV + 1]), jnp.exp(-m_t))
    gdec_row = b_tot - b_row + i_row
    m_new = jnp.maximum(b_tot + m_prev, jnp.max(gdec_row, axis=-1, keepdims=True))
    a = jnp.exp(b_tot + m_prev - m_new)
    kwt = (kt.astype(F32) * jnp.exp(gdec_row - m_new)).astype(BF16)
    c_new = a * c_prev + _dot(kwt, v_ext)
    return h, c_new, m_new


def _mlstm_body(qf_ref, ktf_ref, vf_ref, gf_ref, gtf_ref, qb_ref, ktb_ref, vb_ref, gb_ref, gtb_ref,
                c0_ref, m0_ref, hf_ref, hb_ref, c_ref, m_ref):
    @pl.when(pl.program_id(1) == 0)
    def _():
        c_ref[...] = c0_ref[...]
        m_ref[...] = m0_ref[...]

    L = ML_CHUNK
    r = lax.broadcasted_iota(I32, (L, L), 0)
    s = lax.broadcasted_iota(I32, (L, L), 1)
    mle = r <= s
    mge = r >= s
    mle_b = mle.astype(BF16)
    mge_b = mge.astype(BF16)
    ones_blk = (lax.broadcasted_iota(I32, (L, ML_DV), 1) == 0).astype(BF16)
    dirs = ((qf_ref, ktf_ref, vf_ref, gf_ref, gtf_ref, hf_ref), (qb_ref, ktb_ref, vb_ref, gb_ref, gtb_ref, hb_ref))
    for d, (q_ref, kt_ref, v_ref, g_ref, gt_ref, h_ref) in enumerate(dirs):
        g = g_ref[0]
        gt = gt_ref[0]
        bc = _dot_exact_lhs(mge_b if d == 0 else mle_b, _log_sigmoid(g))
        br = _dot_exact_rhs(_log_sigmoid(gt), mle_b if d == 0 else mge_b)
        mask = mge if d == 0 else mle
        qa = q_ref[0]
        kta = kt_ref[0]
        va = v_ref[0].astype(BF16)
        outs = []
        for h in range(N_HEADS):
            ci = 2 * N_HEADS * d + h
            cf = ci + N_HEADS
            b_row = br[cf:cf + 1, :]
            b_tot = b_row[:, L - 1:L] if d == 0 else b_row[:, 0:1]
            v_ext = jnp.concatenate([va[:, h * ML_DV:(h + 1) * ML_DV], ones_blk], axis=1)
            outs.append(_mlstm_head(
                qa[:, h * ML_DQK:(h + 1) * ML_DQK], kta[h * ML_DQK:(h + 1) * ML_DQK, :], v_ext,
                bc[:, cf:cf + 1], gt[ci:ci + 1, :], b_row, b_tot, mask,
                c_ref[0, d, h], m_ref[0, d, h][:, 0:1]))
        h_ref[0] = jnp.concatenate([o[0] for o in outs], axis=1)
        c_ref[0, d] = jnp.stack([o[1] for o in outs])
        m_ref[0, d] = jnp.stack([jnp.broadcast_to(o[2], (1, LANES)) for o in outs])


def _mlstm(qkc, kt, p_ml, gates, gates_t, c0, m0):
    bsz, n, _ = qkc.shape
    L = ML_CHUNK
    nc = n // L
    hq = SEC // 2
    fwd = lambda b, i: (b, i, 0)
    bwd = lambda b, i: (b, nc - 1 - i, 0)
    st = lambda b, i: (b, 0, 0, 0, 0)
    c_spec = pl.BlockSpec((1, 2, N_HEADS, ML_DQK, 2 * ML_DV), st)
    m_spec = pl.BlockSpec((1, 2, N_HEADS, 1, LANES), st)

    def dir_specs(pos):
        return [pl.BlockSpec((1, L, hq), lambda b, i: (b, pos(i), 0)),
                pl.BlockSpec((1, hq, L), lambda b, i: (b, 0, pos(i))),
                pl.BlockSpec((1, L, SEC), lambda b, i: (b, pos(i), 1)),
                pl.BlockSpec((1, L, LANES), lambda b, i: (b, pos(i), 0)),
                pl.BlockSpec((1, 4 * N_HEADS, L), lambda b, i: (b, 0, pos(i)))]

    return pl.pallas_call(
        _mlstm_body,
        grid=(bsz, nc),
        in_specs=dir_specs(lambda i: i) + dir_specs(lambda i: nc - 1 - i) + [c_spec, m_spec],
        out_specs=[pl.BlockSpec((1, L, SEC), fwd), pl.BlockSpec((1, L, SEC), bwd), c_spec, m_spec],
        out_shape=[jax.ShapeDtypeStruct((bsz, n, SEC), F32), jax.ShapeDtypeStruct((bsz, n, SEC), F32),
                   jax.ShapeDtypeStruct(c0.shape, F32), jax.ShapeDtypeStruct(m0.shape, F32)],
        compiler_params=pltpu.CompilerParams(dimension_semantics=("parallel", "arbitrary"),
                                             vmem_limit_bytes=VMEM_LIMIT),
        name="mlstm",
    )(qkc, kt, p_ml, gates, gates_t, qkc, kt, p_ml, gates, gates_t, c0, m0)


def _attn_body(lam_ref, qt_ref, k_ref, vt_ref, o_ref, qq_s, sa, sb, ma, mb, m_s, acc_s, *, tk, lambda_init):
    qt = qt_ref[0, 0]
    tq = qt.shape[1]
    row = lax.broadcasted_iota(I32, qt.shape, 0)
    zero = jnp.zeros_like(qt)
    qq_s[...] = jnp.concatenate([jnp.where(row < DA_DQK, qt, zero), jnp.where(row >= DA_DQK, qt, zero)], axis=1)
    m_s[...] = jnp.full(m_s.shape, -jnp.inf, F32)
    acc_s[...] = jnp.zeros(acc_s.shape, F32)
    nk = k_ref.shape[1] // tk

    def scores(t, s_ref, mt_ref):
        off = pl.multiple_of(t * tk, LANES)
        s = _dot(k_ref[0, pl.ds(off, tk), :], qq_s[...])
        s_ref[...] = s
        mt_ref[...] = jnp.max(s.reshape(tk // SUBLANES, SUBLANES, 2 * tq).max(axis=0), axis=0, keepdims=True)

    def absorb(t, s_ref, mt_ref):
        off = pl.multiple_of(t * tk, LANES)
        m_old = m_s[...]
        m_new = jnp.maximum(m_old, mt_ref[...])
        p = jnp.exp2(s_ref[...] - m_new)
        alpha = jnp.exp2(m_old - m_new)
        acc_s[...] = alpha * acc_s[...] + _dot(vt_ref[0, 0, :, pl.ds(off, tk)], p.astype(BF16))
        m_s[...] = m_new

    scores(0, sa, ma)
    n_pairs = (nk - 1) // 2

    def pair(i, carry):
        t = 2 * i
        scores(t + 1, sb, mb)
        absorb(t, sa, ma)
        scores(t + 2, sa, ma)
        absorb(t + 1, sb, mb)
        return carry

    lax.fori_loop(0, n_pairs, pair, 0)
    if nk - 1 == 2 * n_pairs:
        absorb(nk - 1, sa, ma)
    else:
        scores(nk - 1, sb, mb)
        absorb(nk - 2, sa, ma)
        absorb(nk - 1, sb, mb)
    lm = lam_ref[...]
    lam = (jnp.exp(jnp.sum(lm[0:1] * lm[1:2], axis=-1, keepdims=True))
           - jnp.exp(jnp.sum(lm[2:3] * lm[3:4], axis=-1, keepdims=True)) + lambda_init)
    acc = acc_s[...]
    o2 = acc[:DA_DV] / acc[DA_DV:DA_DV + 1]
    o_ref[0] = jnp.transpose(o2[:, :tq] - lam * o2[:, tq:])


def _attn(da_lambda, qt, k_all, vt, tq, tk, lambda_init):
    bsz, _, hd, n = qt.shape
    m = k_all.shape[1]
    return pl.pallas_call(
        functools.partial(_attn_body, tk=tk, lambda_init=lambda_init),
        grid=(bsz, N_HEADS, n // tq),
        in_specs=[pl.BlockSpec((4, DA_DQK), lambda b, h, i: (0, 0)),
                  pl.BlockSpec((1, 1, hd, tq), lambda b, h, i: (b, h, 0, i)),
                  pl.BlockSpec((1, m, hd), lambda b, h, i: (b, 0, h)),
                  pl.BlockSpec((1, 1, DA_VROWS, m), lambda b, h, i: (b, h, 0, 0))],
        out_specs=pl.BlockSpec((1, tq, DA_DV), lambda b, h, i: (b, i, h)),
        out_shape=jax.ShapeDtypeStruct((bsz, n, SEC), F32),
        scratch_shapes=[pltpu.VMEM((hd, 2 * tq), BF16),
                        pltpu.VMEM((tk, 2 * tq), F32), pltpu.VMEM((tk, 2 * tq), F32),
                        pltpu.VMEM((1, 2 * tq), F32), pltpu.VMEM((1, 2 * tq), F32),
                        pltpu.VMEM((1, 2 * tq), F32), pltpu.VMEM((DA_VROWS, 2 * tq), F32)],
        compiler_params=pltpu.CompilerParams(dimension_semantics=("parallel", "parallel", "arbitrary"),
                                             vmem_limit_bytes=VMEM_LIMIT),
        name="diff_attn",
    )(da_lambda, qt, k_all, vt)


def _merge_body(hf_ref, hb_ref, o_ref, od_ref, x_ref, wout_ref, mlw_ref, daw_ref, g1_ref, ln1w_ref, ln1b_ref,
                sh2_ref, sc2_ref, wr_ref, h1_ref, u_ref, aff_ref, *, lambda_init, n_experts):
    hsum = hf_ref[0] + hb_ref[0]
    og = o_ref[0]
    od = od_ref[0]
    mlw = mlw_ref[...]
    daw = daw_ref[...]
    parts = []
    for h in range(N_HEADS):
        sl = slice(h * ML_DV, (h + 1) * ML_DV)
        parts.append(_ln_rows(hsum[:, sl]) * mlw[:, sl] * _sigmoid(og[:, sl]))
    for h in range(N_HEADS):
        sl = slice(h * DA_DV, (h + 1) * DA_DV)
        z = od[:, sl]
        zn = z * lax.rsqrt(jnp.mean(z * z, axis=-1, keepdims=True) + LN_EPS)
        parts.append(zn * daw[:, sl] * (1.0 - lambda_init))
    ycat = jnp.concatenate(parts, axis=1).astype(BF16)
    mix = _dot(ycat, wout_ref[...])
    h1 = _ln_rows(DEEPNORM_ALPHA * x_ref[0] + g1_ref[0] * mix) * ln1w_ref[...] + ln1b_ref[...]
    h1_ref[0] = h1
    u = _ln_rows(h1) * (1.0 + sc2_ref[0]) + sh2_ref[0]
    u_ref[0] = u
    uhi, umid, _ = _split3(u)
    whi, wmid, _ = _split3(wr_ref[...])
    logits = _dot(uhi, whi) + _dot(uhi, wmid) + _dot(umid, whi)
    lane = lax.broadcasted_iota(I32, logits.shape, 1)
    logits = jnp.where(lane < n_experts, logits, -jnp.inf)
    e = jnp.exp(logits - jnp.max(logits, axis=-1, keepdims=True))
    aff_ref[0] = e / jnp.sum(e, axis=-1, keepdims=True)


def _merge(hf, hb, p_ml, od, x, w_out, ml_norm_w, da_norm_w, g1, ln1_w, ln1_b, sh2, sc2, w_router_pad,
           tm, lambda_init, n_experts):
    bsz, n, d = x.shape
    row = lambda b, i: (b, i, 0)
    vec = lambda b, i: (0, 0)
    mod = lambda b, i: (b, 0, 0)
    return pl.pallas_call(
        functools.partial(_merge_body, lambda_init=lambda_init, n_experts=n_experts),
        grid=(bsz, n // tm),
        in_specs=[pl.BlockSpec((1, tm, SEC), row), pl.BlockSpec((1, tm, SEC), row),
                  pl.BlockSpec((1, tm, SEC), lambda b, i: (b, i, 2)),
                  pl.BlockSpec((1, tm, SEC), row),
                  pl.BlockSpec((1, tm, d), row),
                  pl.BlockSpec((2 * SEC, d), vec),
                  pl.BlockSpec((1, SEC), vec), pl.BlockSpec((1, SEC), vec),
                  pl.BlockSpec((1, 1, d), mod),
                  pl.BlockSpec((1, d), vec), pl.BlockSpec((1, d), vec),
                  pl.BlockSpec((1, 1, d), mod), pl.BlockSpec((1, 1, d), mod),
                  pl.BlockSpec((d, LANES), vec)],
        out_specs=[pl.BlockSpec((1, tm, d), row), pl.BlockSpec((1, tm, d), row),
                   pl.BlockSpec((1, tm, LANES), row)],
        out_shape=[jax.ShapeDtypeStruct((bsz, n, d), F32), jax.ShapeDtypeStruct((bsz, n, d), F32),
                   jax.ShapeDtypeStruct((bsz, n, LANES), F32)],
        compiler_params=pltpu.CompilerParams(dimension_semantics=("parallel", "parallel"),
                                             vmem_limit_bytes=VMEM_LIMIT),
        name="merge_ln1_router",
    )(hf, hb, p_ml, od, x, w_out, ml_norm_w.reshape(1, SEC), da_norm_w.reshape(1, SEC), g1,
      ln1_w.reshape(1, d), ln1_b.reshape(1, d), sh2, sc2, w_router_pad)


def _cumsum_tokens(maskf, uincl_b, lstrict_b):
    mb = maskf.astype(BF16)
    win = _dot(mb, uincl_b)
    tot = jnp.broadcast_to(win[:, LANES - 1:LANES], win.shape)
    offs = _dot(lstrict_b, tot.astype(BF16))
    return offs + win, tot


def _router_body(a_ref, idx_ref, g_ref, *, cap):
    a = a_ref[0]
    rr = a.shape[0]
    bits = lax.bitcast_convert_type(a, I32)
    cur = jnp.zeros((1, 1), I32)
    for bit in range(30, -1, -1):
        cand = cur | (1 << bit)
        cnt = jnp.sum((bits >= cand).astype(I32), keepdims=True)
        cur = jnp.where(cnt >= cap, cand, cur)
    gt = bits > cur
    eq = bits == cur
    need = (cap - jnp.sum(gt.astype(I32), keepdims=True)).astype(F32)

    li = lax.broadcasted_iota(I32, (LANES, LANES), 0)
    lj = lax.broadcasted_iota(I32, (LANES, LANES), 1)
    uincl_b = (li <= lj).astype(BF16)
    ri = lax.broadcasted_iota(I32, (rr, rr), 0)
    rj = lax.broadcasted_iota(I32, (rr, rr), 1)
    lstrict_b = (rj < ri).astype(BF16)
    rincl_b = (ri <= rj).astype(BF16)

    eqf = eq.astype(F32)
    eq_incl, _ = _cumsum_tokens(eqf, uincl_b, lstrict_b)
    sel = gt | (eq & ((eq_incl - eqf) < need))
    self_ = sel.astype(F32)
    csum, _ = _cumsum_tokens(self_, uincl_b, lstrict_b)

    ones8 = jnp.ones((SUBLANES, LANES), BF16)
    tot_lane = lax.dot_general(ones8, self_.astype(BF16), (((1,), (1,)), ((), ())),
                               preferred_element_type=F32)
    end_lane = _dot(tot_lane.astype(BF16), rincl_b)[0:1]
    beg_lane = end_lane - tot_lane[0:1]
    j = lax.broadcasted_iota(I32, (cap, 1), 0).astype(F32)
    onehot = ((beg_lane <= j) & (end_lane > j)).astype(BF16)
    row_idx = jnp.sum((end_lane <= j).astype(F32), axis=-1, keepdims=True)
    chi = jnp.floor(csum * (1.0 / LANES))
    clo = csum - chi * LANES
    crow = _dot(onehot, chi.astype(BF16)) * LANES + _dot(onehot, clo.astype(BF16))
    lane_idx = jnp.sum((crow <= j).astype(F32), axis=-1, keepdims=True)
    idx_ref[0] = (row_idx * LANES + lane_idx).astype(I32)
    ahi, amid, alo = _split3(a)
    arow = _dot(onehot, ahi) + _dot(onehot, amid) + _dot(onehot, alo)
    lane = lax.broadcasted_iota(I32, (cap, LANES), 1).astype(F32)
    g_ref[0] = jnp.sum(jnp.where(lane == lane_idx, arow, 0.0), axis=-1, keepdims=True)


def _router(aff_t, cap):
    g, rr, _ = aff_t.shape
    return pl.pallas_call(
        functools.partial(_router_body, cap=cap),
        grid=(g,),
        in_specs=[pl.BlockSpec((1, rr, LANES), lambda i: (i, 0, 0))],
        out_specs=[pl.BlockSpec((1, cap, 1), lambda i: (i, 0, 0)), pl.BlockSpec((1, cap, 1), lambda i: (i, 0, 0))],
        out_shape=[jax.ShapeDtypeStruct((g, cap, 1), I32), jax.ShapeDtypeStruct((g, cap, 1), F32)],
        compiler_params=pltpu.CompilerParams(dimension_semantics=("parallel",), vmem_limit_bytes=VMEM_LIMIT),
        name="ec_select",
    )(aff_t)


def _moe_body(idx_ref, g_ref, u_hbm, wg_ref, wu_ref, wd_ref, y_in_hbm, y_hbm, buf, xs, tmp, sem,
              *, cap, n_tok, n_experts, rc):
    del y_in_hbm
    grp = pl.program_id(0)
    h = pl.program_id(1)
    base = (grp // n_experts) * n_tok

    @pl.when(h == 0)
    def _():
        def issue(j, carry):
            row = base + idx_ref[0, 0, j]
            pltpu.make_async_copy(u_hbm.at[pl.ds(row, 1)], buf.at[pl.ds(j, 1)], sem.at[0]).start()
            return carry
        lax.fori_loop(0, cap, issue, 0, unroll=DMA_UNROLL)
        pltpu.make_async_copy(u_hbm.at[pl.ds(0, cap)], buf, sem.at[0]).wait()
        xs[...] = buf[...].astype(BF16)
        buf[...] = jnp.zeros(buf.shape, F32)

    x = xs[...]
    a = _dot(x, wg_ref[0].astype(BF16))
    b = _dot(x, wu_ref[0].astype(BF16))
    hid = (a * _sigmoid(a) * b).astype(BF16)
    buf[...] += _dot(hid, wd_ref[0].astype(BF16))

    @pl.when(h == pl.num_programs(1) - 1)
    def _():
        for c in range(cap // rc):
            def gather(r, carry, c=c):
                row = base + idx_ref[0, 0, c * rc + r]
                pltpu.make_async_copy(y_hbm.at[pl.ds(row, 1)], tmp.at[pl.ds(r, 1)], sem.at[1]).start()
                return carry
            lax.fori_loop(0, rc, gather, 0, unroll=DMA_UNROLL)
            pltpu.make_async_copy(y_hbm.at[pl.ds(0, rc)], tmp, sem.at[1]).wait()
            tmp[...] = tmp[...] + g_ref[0, c * rc:(c + 1) * rc, :] * buf[c * rc:(c + 1) * rc, :]

            def scatter(r, carry, c=c):
                row = base + idx_ref[0, 0, c * rc + r]
                pltpu.make_async_copy(tmp.at[pl.ds(r, 1)], y_hbm.at[pl.ds(row, 1)], sem.at[2]).start()
                return carry
            lax.fori_loop(0, rc, scatter, 0, unroll=DMA_UNROLL)
            pltpu.make_async_copy(tmp, y_hbm.at[pl.ds(0, rc)], sem.at[2]).wait()


def _moe(idx3, gsel, u2, w_gate, w_up, w_down, y0, n_tok, th):
    g, _, cap = idx3.shape
    n_experts, d, de = w_gate.shape
    rc = min(256, cap)
    return pl.pallas_call(
        functools.partial(_moe_body, cap=cap, n_tok=n_tok, n_experts=n_experts, rc=rc),
        grid=(g, de // th),
        in_specs=[pl.BlockSpec((1, 1, cap), lambda i, h: (i, 0, 0), memory_space=pltpu.SMEM),
                  pl.BlockSpec((1, cap, 1), lambda i, h: (i, 0, 0)),
                  pl.BlockSpec(memory_space=pl.ANY),
                  pl.BlockSpec((1, d, th), lambda i, h: (i % n_experts, 0, h)),
                  pl.BlockSpec((1, d, th), lambda i, h: (i % n_experts, 0, h)),
                  pl.BlockSpec((1, th, d), lambda i, h: (i % n_experts, h, 0)),
                  pl.BlockSpec(memory_space=pl.ANY)],
        out_specs=pl.BlockSpec(memory_space=pl.ANY),
        out_shape=jax.ShapeDtypeStruct(y0.shape, F32),
        scratch_shapes=[pltpu.VMEM((cap, d), F32), pltpu.VMEM((cap, d), BF16), pltpu.VMEM((rc, d), F32),
                        pltpu.SemaphoreType.DMA((3,))],
        input_output_aliases={6: 0},
        compiler_params=pltpu.CompilerParams(dimension_semantics=("arbitrary", "arbitrary"),
                                             vmem_limit_bytes=VMEM_LIMIT, disable_bounds_checks=True),
        name="ec_moe",
    )(idx3, gsel, u2, w_gate, w_up, w_down, y0)


def _final_body(h1_ref, y_ref, g2_ref, w_ref, b_ref, o_ref):
    t = DEEPNORM_ALPHA * h1_ref[0] + g2_ref[0] * y_ref[0]
    o_ref[0] = _ln_rows(t) * w_ref[...] + b_ref[...]


def _final(h1, y, g2, ln2_w, ln2_b, tm):
    bsz, n, d = h1.shape
    row = lambda b, i: (b, i, 0)
    return pl.pallas_call(
        _final_body,
        grid=(bsz, n // tm),
        in_specs=[pl.BlockSpec((1, tm, d), row), pl.BlockSpec((1, tm, d), row),
                  pl.BlockSpec((1, 1, d), lambda b, i: (b, 0, 0)),
                  pl.BlockSpec((1, d), lambda b, i: (0, 0)), pl.BlockSpec((1, d), lambda b, i: (0, 0))],
        out_specs=pl.BlockSpec((1, tm, d), row),
        out_shape=jax.ShapeDtypeStruct((bsz, n, d), F32),
        compiler_params=pltpu.CompilerParams(dimension_semantics=("parallel", "parallel"),
                                             vmem_limit_bytes=VMEM_LIMIT),
        name="ln2",
    )(h1, y, g2, ln2_w.reshape(1, d), ln2_b.reshape(1, d))


def _rope_tables(n):
    rows = n // GRID_W
    pos_row = jnp.repeat(jnp.arange(rows, dtype=I32), GRID_W).astype(F32)
    pos_col = jnp.tile(jnp.arange(GRID_W, dtype=I32), rows).astype(F32)
    n_freq = DA_DQK // 4
    inv = ROPE_BASE ** (-jnp.arange(n_freq, dtype=F32) / n_freq)
    ang = jnp.concatenate([pos_row[:, None] * inv, pos_col[:, None] * inv], -1)
    cos, sin = jnp.cos(ang), jnp.sin(ang)
    reps = LANES // DA_DQK
    cosf = jnp.tile(cos, (1, 2 * reps))
    sins = jnp.tile(jnp.concatenate([-sin, sin], -1), (1, reps))
    return cosf, sins


def _pick(n, prefs):
    for t in prefs:
        if n % t == 0:
            return t
    return n


def kernel(x, c, ctx, c_ctx, w_ada, b_ada, w_in, b_gates, ml_conv_w, ml_conv_b, ml_norm_w, da_lambda, da_norm_w,
           w_out, ln1_w, ln1_b, w_router, w_gate, w_up, w_down, ln2_w, ln2_b):
    bsz, n, d = x.shape
    n_ctx = ctx.shape[1]
    n_experts = w_gate.shape[1]
    cap = EC_CAPACITY_FACTOR * n // n_experts
    l = 0
    lambda_init = 0.8 - 0.6 * math.exp(-0.3 * l)

    cin = jnp.zeros((SUBLANES, d), F32).at[:bsz].set(c).at[bsz].set(c_ctx)
    mod = _ada(cin, w_ada[l], b_ada[l])
    sh1, sc1, g1, sh2, sc2, g2 = (mod[:bsz, k * d:(k + 1) * d][:, None, :] for k in range(6))
    csh1, csc1 = (mod[bsz:bsz + 1, k * d:(k + 1) * d][:, None, :] for k in range(2))

    w = w_in[l]
    off_gates = 3 * SEC
    n_gate = 4 * N_HEADS
    w_main = jnp.concatenate([w[:, :off_gates], w[:, off_gates + n_gate:]], axis=1).astype(BF16)
    w_g = jnp.pad(w[:, off_gates:off_gates + n_gate], ((0, 0), (0, LANES - n_gate))).astype(BF16)
    b_g = jnp.pad(b_gates[l], (0, LANES - n_gate)).reshape(1, LANES)
    pml_l, pda_l, gates_l = _proj(x, sh1, sc1, w_main, w_g, b_g, _rope_tables(n), _pick(n, (512, 256, 128)))
    pml_c, pda_c, gates_c = _proj(ctx, csh1, csc1, w_main, w_g, b_g, None, _pick(n_ctx, (512, 256, 128)))

    qkc_l = _conv(pml_l, ml_conv_w[l], ml_conv_b[l], _pick(n, (512, 256, 128)))
    qkc_c = _conv(pml_c, ml_conv_w[l], ml_conv_b[l], _pick(n_ctx, (512, 256, 128)))
    gt_l = jnp.swapaxes(gates_l[:, :, :n_gate], 1, 2)
    gt_c = jnp.swapaxes(gates_c[:, :, :n_gate], 1, 2)
    kt_l = jnp.swapaxes(qkc_l[:, :, SEC // 2:], 1, 2)
    kt_c = jnp.swapaxes(qkc_c[:, :, SEC // 2:], 1, 2)
    c0 = jnp.zeros((bsz, 2, N_HEADS, ML_DQK, 2 * ML_DV), F32)
    m0 = jnp.zeros((bsz, 2, N_HEADS, 1, LANES), F32)
    _, _, c1, m1 = _mlstm(qkc_c, kt_c, pml_c, gates_c, gt_c, c0, m0)
    hf, hb, _, _ = _mlstm(qkc_l, kt_l, pml_l, gates_l, gt_l, c1, m1)

    m_all = n + n_ctx
    k_all = jnp.concatenate([pda_l[:, :, SEC:2 * SEC], pda_c[:, :, SEC:2 * SEC]], axis=1)
    v_all = jnp.concatenate([pda_l[:, :, 2 * SEC:], pda_c[:, :, 2 * SEC:]], axis=1)
    qt = pda_l[:, :, :SEC].reshape(bsz, n, N_HEADS, 2 * DA_DQK).transpose(0, 2, 3, 1)
    vt = v_all.reshape(bsz, m_all, N_HEADS, DA_DV).transpose(0, 2, 3, 1)
    ones_rows = jnp.zeros((bsz, N_HEADS, DA_VROWS - DA_DV, m_all), BF16).at[:, :, 0].set(1.0)
    vt = jnp.concatenate([vt, ones_rows], axis=2)
    od = _attn(da_lambda[l], qt, k_all, vt, _pick(n, (256, 128)), _pick(m_all, (1280, 256, 128)), lambda_init)

    w_router_pad = jnp.pad(w_router[l], ((0, 0), (0, LANES - n_experts)))
    h1, u, aff = _merge(hf, hb, pml_l, od, x, w_out[l].astype(BF16), ml_norm_w[l], da_norm_w[l], g1,
                        ln1_w[l], ln1_b[l], sh2, sc2, w_router_pad, _pick(n, (256, 128)), lambda_init, n_experts)

    aff_t = jnp.swapaxes(aff[:, :, :n_experts], 1, 2).reshape(bsz * n_experts, n // LANES, LANES)
    idx, gsel = _router(aff_t, cap)
    y = _moe(idx.reshape(bsz * n_experts, 1, cap), gsel, u.reshape(bsz * n, d), w_gate[l], w_up[l], w_down[l],
             jnp.zeros((bsz * n, d), F32), n, _pick(w_gate.shape[3], (256, 128)))
    return _final(h1, y.reshape(bsz, n, d), g2, ln2_w[l], ln2_b[l], _pick(n, (512, 256, 128)))
```

```python
import functools
import math

import jax
import jax.numpy as jnp
from jax import lax
from jax.experimental import pallas as pl
from jax.experimental.pallas import tpu as pltpu

F32 = jnp.float32
BF16 = jnp.bfloat16
I32 = jnp.int32

LANES = 128
SUBLANES = 8
GRID_W = 64
N_HEADS = 8
ML_DQK = 64
ML_DV = 128
ML_CONV_W = 5
ML_CHUNK = 128
DA_DQK = 64
DA_DV = 128
SEC = 1024
ROPE_BASE = 10000.0
LN_EPS = 1e-5
DEPTH = 1
DEEPNORM_ALPHA = (2.0 * DEPTH) ** 0.25
EC_CAPACITY_FACTOR = 2
VMEM_LIMIT = 56 * 1024 * 1024
DA_QSCALE = DA_DQK ** -0.5 * math.log2(math.e)
DMA_UNROLL = 8
DA_VROWS = DA_DV + 16


def _sigmoid(x):
    return 1.0 / (1.0 + jnp.exp(-x))


def _log_sigmoid(x):
    return jnp.minimum(x, 0.0) - jnp.log(1.0 + jnp.exp(-jnp.abs(x)))


def _ln_rows(x):
    mu = jnp.mean(x, axis=-1, keepdims=True)
    xc = x - mu
    var = jnp.mean(xc * xc, axis=-1, keepdims=True)
    return xc * lax.rsqrt(var + LN_EPS)


def _split3(x):
    hi = x.astype(BF16)
    r1 = x - hi.astype(F32)
    mid = r1.astype(BF16)
    lo = (r1 - mid.astype(F32)).astype(BF16)
    return hi, mid, lo


def _dot(a, b):
    return jnp.dot(a, b, preferred_element_type=F32)


def _dot_exact_lhs(a_bf16, x_f32):
    hi, mid, lo = _split3(x_f32)
    return _dot(a_bf16, hi) + _dot(a_bf16, mid) + _dot(a_bf16, lo)


def _dot_exact_rhs(x_f32, a_bf16):
    hi, mid, lo = _split3(x_f32)
    return _dot(hi, a_bf16) + _dot(mid, a_bf16) + _dot(lo, a_bf16)


def _ada_body(c_ref, w_ref, b_ref, o_ref):
    c = c_ref[...]
    a = c * _sigmoid(c)
    hi, mid, lo = _split3(a)
    w = w_ref[...]
    whi, wmid, wlo = _split3(w)
    acc = _dot(hi, whi) + _dot(hi, wmid) + _dot(mid, whi)
    acc = acc + _dot(mid, wmid) + _dot(hi, wlo) + _dot(lo, whi)
    o_ref[...] = acc + b_ref[...]


def _ada(cin, w, b):
    rows, d = cin.shape
    n = w.shape[1]
    tn = _pick(n, (1024, 512, 256, 128))
    return pl.pallas_call(
        _ada_body,
        grid=(n // tn,),
        in_specs=[pl.BlockSpec((rows, d), lambda j: (0, 0)),
                  pl.BlockSpec((d, tn), lambda j: (0, j)),
                  pl.BlockSpec((1, tn), lambda j: (0, j))],
        out_specs=pl.BlockSpec((rows, tn), lambda j: (0, j)),
        out_shape=jax.ShapeDtypeStruct((rows, n), F32),
        compiler_params=pltpu.CompilerParams(dimension_semantics=("arbitrary",),
                                             vmem_limit_bytes=VMEM_LIMIT),
        name="ada_mod",
    )(cin, w, b.reshape(1, n))


def _rope_tile(acc, cosf, sins):
    outs = []
    lane = lax.broadcasted_iota(I32, (acc.shape[0], LANES), 1)
    first = (lane & (DA_DQK - 1)) < (DA_DQK // 2)
    for t in range(acc.shape[1] // LANES):
        a = acc[:, t * LANES:(t + 1) * LANES]
        partner = jnp.where(first, pltpu.roll(a, LANES - DA_DQK // 2, 1), pltpu.roll(a, DA_DQK // 2, 1))
        outs.append(a * cosf + partner * sins)
    return jnp.concatenate(outs, axis=1)


def _proj_body(*refs, rope):
    if rope:
        (x_ref, sh_ref, sc_ref, w_ref, wg_ref, bg_ref, cos_ref, sin_ref,
         pml_ref, pda_ref, g_ref, xn_ref) = refs
    else:
        (x_ref, sh_ref, sc_ref, w_ref, wg_ref, bg_ref, pml_ref, pda_ref, g_ref, xn_ref) = refs
    j = pl.program_id(2)

    @pl.when(j == 0)
    def _():
        y = _ln_rows(x_ref[0]) * (1.0 + sc_ref[0]) + sh_ref[0]
        yb = y.astype(BF16)
        xn_ref[...] = yb
        g_ref[0] = _dot(yb, wg_ref[...]) + bg_ref[...]

    acc = _dot(xn_ref[...], w_ref[...])

    @pl.when(j < 3)
    def _():
        pml_ref[0] = acc

    @pl.when(j == 3)
    def _():
        q = _rope_tile(acc, cos_ref[...], sin_ref[...]) if rope else acc
        pda_ref[0] = (q * DA_QSCALE).astype(BF16)

    @pl.when(j == 4)
    def _():
        k = _rope_tile(acc, cos_ref[...], sin_ref[...]) if rope else acc
        pda_ref[0] = k.astype(BF16)

    @pl.when(j == 5)
    def _():
        pda_ref[0] = acc.astype(BF16)


def _proj(x, sh, sc, w_main, w_g, b_g, rope_tabs, tm):
    bsz, n, d = x.shape
    per_batch = sh.shape[0] == bsz
    rope = rope_tabs is not None
    mod_map = (lambda b, i, j: (b, 0, 0)) if per_batch else (lambda b, i, j: (0, 0, 0))
    in_specs = [pl.BlockSpec((1, tm, d), lambda b, i, j: (b, i, 0)),
                pl.BlockSpec((1, 1, d), mod_map),
                pl.BlockSpec((1, 1, d), mod_map),
                pl.BlockSpec((d, SEC), lambda b, i, j: (0, j)),
                pl.BlockSpec((d, LANES), lambda b, i, j: (0, 0)),
                pl.BlockSpec((1, LANES), lambda b, i, j: (0, 0))]
    args = [x, sh, sc, w_main, w_g, b_g]
    if rope:
        in_specs += [pl.BlockSpec((tm, LANES), lambda b, i, j: (i, 0)),
                     pl.BlockSpec((tm, LANES), lambda b, i, j: (i, 0))]
        args += list(rope_tabs)
    return pl.pallas_call(
        functools.partial(_proj_body, rope=rope),
        grid=(bsz, n // tm, 6),
        in_specs=in_specs,
        out_specs=[pl.BlockSpec((1, tm, SEC), lambda b, i, j: (b, i, jnp.minimum(j, 2))),
                   pl.BlockSpec((1, tm, SEC), lambda b, i, j: (b, i, jnp.maximum(j - 3, 0))),
                   pl.BlockSpec((1, tm, LANES), lambda b, i, j: (b, i, 0))],
        out_shape=[jax.ShapeDtypeStruct((bsz, n, 3 * SEC), F32),
                   jax.ShapeDtypeStruct((bsz, n, 3 * SEC), BF16),
                   jax.ShapeDtypeStruct((bsz, n, LANES), F32)],
        scratch_shapes=[pltpu.VMEM((tm, d), BF16)],
        compiler_params=pltpu.CompilerParams(dimension_semantics=("parallel", "parallel", "arbitrary"),
                                             vmem_limit_bytes=VMEM_LIMIT),
        name="in_proj_rope" if rope else "in_proj",
    )(*args)


def _conv_body(prev_ref, cur_ref, next_ref, w_ref, b_ref, s_ref, o_ref):
    i = pl.program_id(1)
    last = pl.num_programs(1) - 1
    cur = cur_ref[0]
    t = cur.shape[0]
    prev = jnp.where(i > 0, prev_ref[0], 0.0)
    nxt = jnp.where(i < last, next_ref[0], 0.0)
    ext = jnp.concatenate([prev, cur, nxt], axis=0)
    acc = jnp.zeros_like(cur) + b_ref[...]
    half = ML_CONV_W // 2
    for k in range(ML_CONV_W):
        off = SUBLANES - half + k
        acc = acc + ext[off:off + t] * w_ref[k:k + 1, :]
    y = acc * _sigmoid(acc)
    o_ref[0] = (y * s_ref[...]).astype(BF16)


def _conv(p_ml, conv_w, conv_b, tt):
    bsz, n, _ = p_ml.shape
    nb8 = n // SUBLANES
    r = tt // SUBLANES
    scale = jnp.concatenate([jnp.full((1, SEC // 2), ML_DQK ** -0.5, F32), jnp.ones((1, SEC // 2), F32)], axis=1)
    return pl.pallas_call(
        _conv_body,
        grid=(bsz, n // tt),
        in_specs=[pl.BlockSpec((1, SUBLANES, SEC), lambda b, i: (b, jnp.maximum(i * r - 1, 0), 0)),
                  pl.BlockSpec((1, tt, SEC), lambda b, i: (b, i, 0)),
                  pl.BlockSpec((1, SUBLANES, SEC), lambda b, i: (b, jnp.minimum((i + 1) * r, nb8 - 1), 0)),
                  pl.BlockSpec((ML_CONV_W, SEC), lambda b, i: (0, 0)),
                  pl.BlockSpec((1, SEC), lambda b, i: (0, 0)),
                  pl.BlockSpec((1, SEC), lambda b, i: (0, 0))],
        out_specs=pl.BlockSpec((1, tt, SEC), lambda b, i: (b, i, 0)),
        out_shape=jax.ShapeDtypeStruct((bsz, n, SEC), BF16),
        compiler_params=pltpu.CompilerParams(dimension_semantics=("parallel", "parallel"),
                                             vmem_limit_bytes=VMEM_LIMIT),
        name="ml_conv",
    )(p_ml, p_ml, p_ml, conv_w, conv_b.reshape(1, SEC), scale)


def _mlstm_head(q, kt, v_ext, b_col, i_row, b_row, b_tot, mask, c_prev, m_prev):
    dmat = jnp.where(mask, b_col + (i_row - b_row), -jnp.inf)
    m_inter = b_col + m_prev
    m_t = jnp.maximum(m_inter, jnp.max(dmat, axis=-1, keepdims=True))
    w_inter = jnp.exp(m_inter - m_t)
    w = jnp.exp(dmat - m_t) * _dot(q, kt)
    ext = w_inter * _dot(q, c_prev.astype(BF16)) + _dot(w.astype(BF16), v_ext)
    h = ext[:, :ML_DV] / jnp.maximum(jnp.abs(ext[:, ML_DV:ML_DV + 1]), jnp.exp(-m_t))
    gdec_row = b_tot - b_row + i_row
    m_new = jnp.maximum(b_tot + m_prev, jnp.max(gdec_row, axis=-1, keepdims=True))
    a = jnp.exp(b_tot + m_prev - m_new)
    kwt = (kt.astype(F32) * jnp.exp(gdec_row - m_new)).astype(BF16)
    c_new = a * c_prev + _dot(kwt, v_ext)
    return h, c_new, m_new


def _mlstm_body(qf_ref, ktf_ref, vf_ref, gf_ref, gtf_ref, qb_ref, ktb_ref, vb_ref, gb_ref, gtb_ref,
                c0_ref, m0_ref, hf_ref, hb_ref, c_ref, m_ref):
    @pl.when(pl.program_id(1) == 0)
    def _():
        c_ref[...] = c0_ref[...]
        m_ref[...] = m0_ref[...]

    L = ML_CHUNK
    r = lax.broadcasted_iota(I32, (L, L), 0)
    s = lax.broadcasted_iota(I32, (L, L), 1)
    mle = r <= s
    mge = r >= s
    mle_b = mle.astype(BF16)
    mge_b = mge.astype(BF16)
    ones_blk = (lax.broadcasted_iota(I32, (L, ML_DV), 1) == 0).astype(BF16)
    dirs = ((qf_ref, ktf_ref, vf_ref, gf_ref, gtf_ref, hf_ref), (qb_ref, ktb_ref, vb_ref, gb_ref, gtb_ref, hb_ref))
    for d, (q_ref, kt_ref, v_ref, g_ref, gt_ref, h_ref) in enumerate(dirs):
        g = g_ref[0]
        gt = gt_ref[0]
        bc = _dot_exact_lhs(mge_b if d == 0 else mle_b, _log_sigmoid(g))
        br = _dot_exact_rhs(_log_sigmoid(gt), mle_b if d == 0 else mge_b)
        mask = mge if d == 0 else mle
        qa = q_ref[0]
        kta = kt_ref[0]
        va = v_ref[0].astype(BF16)
        outs = []
        for h in range(N_HEADS):
            ci = 2 * N_HEADS * d + h
            cf = ci + N_HEADS
            b_row = br[cf:cf + 1, :]
            b_tot = b_row[:, L - 1:L] if d == 0 else b_row[:, 0:1]
            v_ext = jnp.concatenate([va[:, h * ML_DV:(h + 1) * ML_DV], ones_blk], axis=1)
            outs.append(_mlstm_head(
                qa[:, h * ML_DQK:(h + 1) * ML_DQK], kta[h * ML_DQK:(h + 1) * ML_DQK, :], v_ext,
                bc[:, cf:cf + 1], gt[ci:ci + 1, :], b_row, b_tot, mask,
                c_ref[0, d, h], m_ref[0, d, h][:, 0:1]))
        h_ref[0] = jnp.concatenate([o[0] for o in outs], axis=1)
        c_ref[0, d] = jnp.stack([o[1] for o in outs])
        m_ref[0, d] = jnp.stack([jnp.broadcast_to(o[2], (1, LANES)) for o in outs])


def _mlstm(qkc, kt, p_ml, gates, gates_t, c0, m0):
    bsz, n, _ = qkc.shape
    L = ML_CHUNK
    nc = n // L
    hq = SEC // 2
    fwd = lambda b, i: (b, i, 0)
    bwd = lambda b, i: (b, nc - 1 - i, 0)
    st = lambda b, i: (b, 0, 0, 0, 0)
    c_spec = pl.BlockSpec((1, 2, N_HEADS, ML_DQK, 2 * ML_DV), st)
    m_spec = pl.BlockSpec((1, 2, N_HEADS, 1, LANES), st)

    def dir_specs(pos):
        return [pl.BlockSpec((1, L, hq), lambda b, i: (b, pos(i), 0)),
                pl.BlockSpec((1, hq, L), lambda b, i: (b, 0, pos(i))),
                pl.BlockSpec((1, L, SEC), lambda b, i: (b, pos(i), 1)),
                pl.BlockSpec((1, L, LANES), lambda b, i: (b, pos(i), 0)),
                pl.BlockSpec((1, 4 * N_HEADS, L), lambda b, i: (b, 0, pos(i)))]

    return pl.pallas_call(
        _mlstm_body,
        grid=(bsz, nc),
        in_specs=dir_specs(lambda i: i) + dir_specs(lambda i: nc - 1 - i) + [c_spec, m_spec],
        out_specs=[pl.BlockSpec((1, L, SEC), fwd), pl.BlockSpec((1, L, SEC), bwd), c_spec, m_spec],
        out_shape=[jax.ShapeDtypeStruct((bsz, n, SEC), F32), jax.ShapeDtypeStruct((bsz, n, SEC), F32),
                   jax.ShapeDtypeStruct(c0.shape, F32), jax.ShapeDtypeStruct(m0.shape, F32)],
        compiler_params=pltpu.CompilerParams(dimension_semantics=("parallel", "arbitrary"),
                                             vmem_limit_bytes=VMEM_LIMIT),
        name="mlstm",
    )(qkc, kt, p_ml, gates, gates_t, qkc, kt, p_ml, gates, gates_t, c0, m0)


def _attn_body(lam_ref, qt_ref, k_ref, vt_ref, o_ref, sa, sb, ma, mb, m_s, acc_s, *, tq, tk, lambda_init):
    nk = k_ref.shape[1] // tk
    total = (qt_ref.shape[3] // tq) * nk
    lm = lam_ref[...]
    lam = (jnp.exp(jnp.sum(lm[0:1] * lm[1:2], axis=-1, keepdims=True))
           - jnp.exp(jnp.sum(lm[2:3] * lm[3:4], axis=-1, keepdims=True)) + lambda_init)
    acc_s[...] = jnp.zeros(acc_s.shape, F32)
    m_s[...] = jnp.full(m_s.shape, -jnp.inf, F32)

    def scores(f, s_ref, mt_ref):
        f = jnp.minimum(f, total - 1)
        qoff = pl.multiple_of((f // nk) * tq, LANES)
        koff = pl.multiple_of((f % nk) * tk, LANES)
        qt = qt_ref[0, 0, :, pl.ds(qoff, tq)]
        row = lax.broadcasted_iota(I32, qt.shape, 0)
        zero = jnp.zeros_like(qt)
        qq = jnp.concatenate([jnp.where(row < DA_DQK, qt, zero), jnp.where(row >= DA_DQK, qt, zero)], axis=1)
        s = _dot(k_ref[0, pl.ds(koff, tk), :], qq)
        s_ref[...] = s
        mt_ref[...] = jnp.max(s.reshape(tk // SUBLANES, SUBLANES, 2 * tq).max(axis=0), axis=0, keepdims=True)

    def absorb(f, s_ref, mt_ref):
        t = f % nk
        koff = pl.multiple_of(t * tk, LANES)
        m_old = jnp.where(t == 0, -jnp.inf, m_s[...])
        m_new = jnp.maximum(m_old, mt_ref[...])
        p = jnp.exp2(s_ref[...] - m_new)
        alpha = jnp.exp2(m_old - m_new)
        acc = alpha * acc_s[...] + _dot(vt_ref[0, 0, :, pl.ds(koff, tk)], p.astype(BF16))
        acc_s[...] = acc
        m_s[...] = m_new

        @pl.when(t == nk - 1)
        def _():
            o2 = acc[:DA_DV] / acc[DA_DV:DA_DV + 1]
            qoff = pl.multiple_of((f // nk) * tq, LANES)
            o_ref[0, pl.ds(qoff, tq), :] = jnp.transpose(o2[:, :tq] - lam * o2[:, tq:])

    scores(0, sa, ma)

    def pair(j, carry):
        f = 2 * j
        scores(f + 1, sb, mb)
        absorb(f, sa, ma)
        scores(f + 2, sa, ma)
        absorb(f + 1, sb, mb)
        return carry

    lax.fori_loop(0, total // 2, pair, 0)
    if total % 2:
        absorb(total - 1, sa, ma)


def _attn(da_lambda, qt, k_all, vt, tq, tk, lambda_init):
    bsz, _, hd, n = qt.shape
    m = k_all.shape[1]
    return pl.pallas_call(
        functools.partial(_attn_body, tq=tq, tk=tk, lambda_init=lambda_init),
        grid=(bsz, N_HEADS),
        in_specs=[pl.BlockSpec((4, DA_DQK), lambda b, h: (0, 0)),
                  pl.BlockSpec((1, 1, hd, n), lambda b, h: (b, h, 0, 0)),
                  pl.BlockSpec((1, m, hd), lambda b, h: (b, 0, h)),
                  pl.BlockSpec((1, 1, DA_VROWS, m), lambda b, h: (b, h, 0, 0))],
        out_specs=pl.BlockSpec((1, n, DA_DV), lambda b, h: (b, 0, h)),
        out_shape=jax.ShapeDtypeStruct((bsz, n, SEC), F32),
        scratch_shapes=[pltpu.VMEM((tk, 2 * tq), F32), pltpu.VMEM((tk, 2 * tq), F32),
                        pltpu.VMEM((1, 2 * tq), F32), pltpu.VMEM((1, 2 * tq), F32),
                        pltpu.VMEM((1, 2 * tq), F32), pltpu.VMEM((DA_VROWS, 2 * tq), F32)],
        compiler_params=pltpu.CompilerParams(dimension_semantics=("parallel", "parallel"),
                                             vmem_limit_bytes=VMEM_LIMIT),
        name="diff_attn",
    )(da_lambda, qt, k_all, vt)


def _merge_body(hf_ref, hb_ref, o_ref, od_ref, x_ref, wout_ref, mlw_ref, daw_ref, g1_ref, ln1w_ref, ln1b_ref,
                sh2_ref, sc2_ref, wr_ref, h1_ref, u_ref, aff_ref, *, lambda_init, n_experts):
    hsum = hf_ref[0] + hb_ref[0]
    og = o_ref[0]
    od = od_ref[0]
    mlw = mlw_ref[...]
    daw = daw_ref[...]
    parts = []
    for h in range(N_HEADS):
        sl = slice(h * ML_DV, (h + 1) * ML_DV)
        parts.append(_ln_rows(hsum[:, sl]) * mlw[:, sl] * _sigmoid(og[:, sl]))
    for h in range(N_HEADS):
        sl = slice(h * DA_DV, (h + 1) * DA_DV)
        z = od[:, sl]
        zn = z * lax.rsqrt(jnp.mean(z * z, axis=-1, keepdims=True) + LN_EPS)
        parts.append(zn * daw[:, sl] * (1.0 - lambda_init))
    ycat = jnp.concatenate(parts, axis=1).astype(BF16)
    mix = _dot(ycat, wout_ref[...])
    h1 = _ln_rows(DEEPNORM_ALPHA * x_ref[0] + g1_ref[0] * mix) * ln1w_ref[...] + ln1b_ref[...]
    h1_ref[0] = h1
    u = _ln_rows(h1) * (1.0 + sc2_ref[0]) + sh2_ref[0]
    u_ref[0] = u
    uhi, umid, _ = _split3(u)
    whi, wmid, _ = _split3(wr_ref[...])
    logits = _dot(uhi, whi) + _dot(uhi, wmid) + _dot(umid, whi)
    lane = lax.broadcasted_iota(I32, logits.shape, 1)
    logits = jnp.where(lane < n_experts, logits, -jnp.inf)
    e = jnp.exp(logits - jnp.max(logits, axis=-1, keepdims=True))
    aff_ref[0] = e / jnp.sum(e, axis=-1, keepdims=True)


def _merge(hf, hb, p_ml, od, x, w_out, ml_norm_w, da_norm_w, g1, ln1_w, ln1_b, sh2, sc2, w_router_pad,
           tm, lambda_init, n_experts):
    bsz, n, d = x.shape
    row = lambda b, i: (b, i, 0)
    vec = lambda b, i: (0, 0)
    mod = lambda b, i: (b, 0, 0)
    return pl.pallas_call(
        functools.partial(_merge_body, lambda_init=lambda_init, n_experts=n_experts),
        grid=(bsz, n // tm),
        in_specs=[pl.BlockSpec((1, tm, SEC), row), pl.BlockSpec((1, tm, SEC), row),
                  pl.BlockSpec((1, tm, SEC), lambda b, i: (b, i, 2)),
                  pl.BlockSpec((1, tm, SEC), row),
                  pl.BlockSpec((1, tm, d), row),
                  pl.BlockSpec((2 * SEC, d), vec),
                  pl.BlockSpec((1, SEC), vec), pl.BlockSpec((1, SEC), vec),
                  pl.BlockSpec((1, 1, d), mod),
                  pl.BlockSpec((1, d), vec), pl.BlockSpec((1, d), vec),
                  pl.BlockSpec((1, 1, d), mod), pl.BlockSpec((1, 1, d), mod),
                  pl.BlockSpec((d, LANES), vec)],
        out_specs=[pl.BlockSpec((1, tm, d), row), pl.BlockSpec((1, tm, d), row),
                   pl.BlockSpec((1, tm, LANES), row)],
        out_shape=[jax.ShapeDtypeStruct((bsz, n, d), F32), jax.ShapeDtypeStruct((bsz, n, d), F32),
                   jax.ShapeDtypeStruct((bsz, n, LANES), F32)],
        compiler_params=pltpu.CompilerParams(dimension_semantics=("parallel", "parallel"),
                                             vmem_limit_bytes=VMEM_LIMIT),
        name="merge_ln1_router",
    )(hf, hb, p_ml, od, x, w_out, ml_norm_w.reshape(1, SEC), da_norm_w.reshape(1, SEC), g1,
      ln1_w.reshape(1, d), ln1_b.reshape(1, d), sh2, sc2, w_router_pad)


def _cumsum_tokens(maskf, uincl_b, lstrict_b):
    mb = maskf.astype(BF16)
    win = _dot(mb, uincl_b)
    tot = jnp.broadcast_to(win[:, LANES - 1:LANES], win.shape)
    offs = _dot(lstrict_b, tot.astype(BF16))
    return offs + win, tot


def _router_body(a_ref, idx_ref, g_ref, *, cap):
    a = a_ref[0]
    rr = a.shape[0]
    bits = lax.bitcast_convert_type(a, I32)
    cur = jnp.zeros((1, 1), I32)
    for bit in range(30, -1, -1):
        cand = cur | (1 << bit)
        cnt = jnp.sum((bits >= cand).astype(I32), keepdims=True)
        cur = jnp.where(cnt >= cap, cand, cur)
    gt = bits > cur
    eq = bits == cur
    need = (cap - jnp.sum(gt.astype(I32), keepdims=True)).astype(F32)

    li = lax.broadcasted_iota(I32, (LANES, LANES), 0)
    lj = lax.broadcasted_iota(I32, (LANES, LANES), 1)
    uincl_b = (li <= lj).astype(BF16)
    ri = lax.broadcasted_iota(I32, (rr, rr), 0)
    rj = lax.broadcasted_iota(I32, (rr, rr), 1)
    lstrict_b = (rj < ri).astype(BF16)
    rincl_b = (ri <= rj).astype(BF16)

    eqf = eq.astype(F32)
    eq_incl, _ = _cumsum_tokens(eqf, uincl_b, lstrict_b)
    sel = gt | (eq & ((eq_incl - eqf) < need))
    self_ = sel.astype(F32)
    csum, _ = _cumsum_tokens(self_, uincl_b, lstrict_b)

    ones8 = jnp.ones((SUBLANES, LANES), BF16)
    tot_lane = lax.dot_general(ones8, self_.astype(BF16), (((1,), (1,)), ((), ())),
                               preferred_element_type=F32)
    end_lane = _dot(tot_lane.astype(BF16), rincl_b)[0:1]
    beg_lane = end_lane - tot_lane[0:1]
    j = lax.broadcasted_iota(I32, (cap, 1), 0).astype(F32)
    onehot = ((beg_lane <= j) & (end_lane > j)).astype(BF16)
    row_idx = jnp.sum((end_lane <= j).astype(F32), axis=-1, keepdims=True)
    chi = jnp.floor(csum * (1.0 / LANES))
    clo = csum - chi * LANES
    crow = _dot(onehot, chi.astype(BF16)) * LANES + _dot(onehot, clo.astype(BF16))
    lane_idx = jnp.sum((crow <= j).astype(F32), axis=-1, keepdims=True)
    idx_ref[0] = (row_idx * LANES + lane_idx).astype(I32)
    ahi, amid, alo = _split3(a)
    arow = _dot(onehot, ahi) + _dot(onehot, amid) + _dot(onehot, alo)
    lane = lax.broadcasted_iota(I32, (cap, LANES), 1).astype(F32)
    g_ref[0] = jnp.sum(jnp.where(lane == lane_idx, arow, 0.0), axis=-1, keepdims=True)


def _router(aff_t, cap):
    g, rr, _ = aff_t.shape
    return pl.pallas_call(
        functools.partial(_router_body, cap=cap),
        grid=(g,),
        in_specs=[pl.BlockSpec((1, rr, LANES), lambda i: (i, 0, 0))],
        out_specs=[pl.BlockSpec((1, cap, 1), lambda i: (i, 0, 0)), pl.BlockSpec((1, cap, 1), lambda i: (i, 0, 0))],
        out_shape=[jax.ShapeDtypeStruct((g, cap, 1), I32), jax.ShapeDtypeStruct((g, cap, 1), F32)],
        compiler_params=pltpu.CompilerParams(dimension_semantics=("parallel",), vmem_limit_bytes=VMEM_LIMIT),
        name="ec_select",
    )(aff_t)


def _moe_body(idx_ref, g_ref, u_hbm, wg_ref, wu_ref, wd_ref, y_in_hbm, y_hbm, buf, xs, tmp, sem,
              *, cap, n_tok, n_experts, rc):
    del y_in_hbm
    grp = pl.program_id(0)
    h = pl.program_id(1)
    base = (grp // n_experts) * n_tok

    @pl.when(h == 0)
    def _():
        def issue(j, carry):
            row = base + idx_ref[0, 0, j]
            pltpu.make_async_copy(u_hbm.at[pl.ds(row, 1)], buf.at[pl.ds(j, 1)], sem.at[0]).start()
            return carry
        lax.fori_loop(0, cap, issue, 0, unroll=DMA_UNROLL)
        pltpu.make_async_copy(u_hbm.at[pl.ds(0, cap)], buf, sem.at[0]).wait()
        xs[...] = buf[...].astype(BF16)
        buf[...] = jnp.zeros(buf.shape, F32)

    x = xs[...]
    a = _dot(x, wg_ref[0].astype(BF16))
    b = _dot(x, wu_ref[0].astype(BF16))
    hid = (a * _sigmoid(a) * b).astype(BF16)
    buf[...] += _dot(hid, wd_ref[0].astype(BF16))

    def y_rows(c, slot, to_hbm):
        def issue(r, carry):
            row = base + idx_ref[0, 0, c * rc + r]
            hbm_row = y_hbm.at[pl.ds(row, 1)]
            vmem_row = tmp.at[slot, pl.ds(r, 1)]
            if to_hbm:
                pltpu.make_async_copy(vmem_row, hbm_row, sem.at[3 + slot]).start()
            else:
                pltpu.make_async_copy(hbm_row, vmem_row, sem.at[1 + slot]).start()
            return carry
        lax.fori_loop(0, rc, issue, 0, unroll=DMA_UNROLL)

    def y_wait(slot, to_hbm):
        hbm_rows = y_hbm.at[pl.ds(0, rc)]
        if to_hbm:
            pltpu.make_async_copy(tmp.at[slot], hbm_rows, sem.at[3 + slot]).wait()
        else:
            pltpu.make_async_copy(hbm_rows, tmp.at[slot], sem.at[1 + slot]).wait()

    @pl.when(h == pl.num_programs(1) - 1)
    def _():
        n_chunks = cap // rc
        y_rows(0, 0, False)
        for c in range(n_chunks):
            slot = c % 2
            if c + 1 < n_chunks:
                if c >= 1:
                    y_wait(1 - slot, True)
                y_rows(c + 1, 1 - slot, False)
            y_wait(slot, False)
            tmp[slot] = tmp[slot] + g_ref[0, c * rc:(c + 1) * rc, :] * buf[c * rc:(c + 1) * rc, :]
            y_rows(c, slot, True)
        for c in range(max(n_chunks - 2, 0), n_chunks):
            y_wait(c % 2, True)


def _moe(idx3, gsel, u2, w_gate, w_up, w_down, y0, n_tok, th):
    g, _, cap = idx3.shape
    n_experts, d, de = w_gate.shape
    rc = min(256, max(cap // 4, SUBLANES))
    return pl.pallas_call(
        functools.partial(_moe_body, cap=cap, n_tok=n_tok, n_experts=n_experts, rc=rc),
        grid=(g, de // th),
        in_specs=[pl.BlockSpec((1, 1, cap), lambda i, h: (i, 0, 0), memory_space=pltpu.SMEM),
                  pl.BlockSpec((1, cap, 1), lambda i, h: (i, 0, 0)),
                  pl.BlockSpec(memory_space=pl.ANY),
                  pl.BlockSpec((1, d, th), lambda i, h: (i % n_experts, 0, h)),
                  pl.BlockSpec((1, d, th), lambda i, h: (i % n_experts, 0, h)),
                  pl.BlockSpec((1, th, d), lambda i, h: (i % n_experts, h, 0)),
                  pl.BlockSpec(memory_space=pl.ANY)],
        out_specs=pl.BlockSpec(memory_space=pl.ANY),
        out_shape=jax.ShapeDtypeStruct(y0.shape, F32),
        scratch_shapes=[pltpu.VMEM((cap, d), F32), pltpu.VMEM((cap, d), BF16), pltpu.VMEM((2, rc, d), F32),
                        pltpu.SemaphoreType.DMA((5,))],
        input_output_aliases={6: 0},
        compiler_params=pltpu.CompilerParams(dimension_semantics=("arbitrary", "arbitrary"),
                                             vmem_limit_bytes=VMEM_LIMIT, disable_bounds_checks=True),
        name="ec_moe",
    )(idx3, gsel, u2, w_gate, w_up, w_down, y0)


def _final_body(h1_ref, y_ref, g2_ref, w_ref, b_ref, o_ref):
    t = DEEPNORM_ALPHA * h1_ref[0] + g2_ref[0] * y_ref[0]
    o_ref[0] = _ln_rows(t) * w_ref[...] + b_ref[...]


def _final(h1, y, g2, ln2_w, ln2_b, tm):
    bsz, n, d = h1.shape
    row = lambda b, i: (b, i, 0)
    return pl.pallas_call(
        _final_body,
        grid=(bsz, n // tm),
        in_specs=[pl.BlockSpec((1, tm, d), row), pl.BlockSpec((1, tm, d), row),
                  pl.BlockSpec((1, 1, d), lambda b, i: (b, 0, 0)),
                  pl.BlockSpec((1, d), lambda b, i: (0, 0)), pl.BlockSpec((1, d), lambda b, i: (0, 0))],
        out_specs=pl.BlockSpec((1, tm, d), row),
        out_shape=jax.ShapeDtypeStruct((bsz, n, d), F32),
        compiler_params=pltpu.CompilerParams(dimension_semantics=("parallel", "parallel"),
                                             vmem_limit_bytes=VMEM_LIMIT),
        name="ln2",
    )(h1, y, g2, ln2_w.reshape(1, d), ln2_b.reshape(1, d))


def _rope_tables(n):
    rows = n // GRID_W
    pos_row = jnp.repeat(jnp.arange(rows, dtype=I32), GRID_W).astype(F32)
    pos_col = jnp.tile(jnp.arange(GRID_W, dtype=I32), rows).astype(F32)
    n_freq = DA_DQK // 4
    inv = ROPE_BASE ** (-jnp.arange(n_freq, dtype=F32) / n_freq)
    ang = jnp.concatenate([pos_row[:, None] * inv, pos_col[:, None] * inv], -1)
    cos, sin = jnp.cos(ang), jnp.sin(ang)
    reps = LANES // DA_DQK
    cosf = jnp.tile(cos, (1, 2 * reps))
    sins = jnp.tile(jnp.concatenate([-sin, sin], -1), (1, reps))
    return cosf, sins


def _pick(n, prefs):
    for t in prefs:
        if n % t == 0:
            return t
    return n


def kernel(x, c, ctx, c_ctx, w_ada, b_ada, w_in, b_gates, ml_conv_w, ml_conv_b, ml_norm_w, da_lambda, da_norm_w,
           w_out, ln1_w, ln1_b, w_router, w_gate, w_up, w_down, ln2_w, ln2_b):
    bsz, n, d = x.shape
    n_ctx = ctx.shape[1]
    n_experts = w_gate.shape[1]
    cap = EC_CAPACITY_FACTOR * n // n_experts
    l = 0
    lambda_init = 0.8 - 0.6 * math.exp(-0.3 * l)

    cin = jnp.zeros((SUBLANES, d), F32).at[:bsz].set(c).at[bsz].set(c_ctx)
    mod = _ada(cin, w_ada[l], b_ada[l])
    sh1, sc1, g1, sh2, sc2, g2 = (mod[:bsz, k * d:(k + 1) * d][:, None, :] for k in range(6))
    csh1, csc1 = (mod[bsz:bsz + 1, k * d:(k + 1) * d][:, None, :] for k in range(2))

    w = w_in[l]
    off_gates = 3 * SEC
    n_gate = 4 * N_HEADS
    w_main = jnp.concatenate([w[:, :off_gates], w[:, off_gates + n_gate:]], axis=1).astype(BF16)
    w_g = jnp.pad(w[:, off_gates:off_gates + n_gate], ((0, 0), (0, LANES - n_gate))).astype(BF16)
    b_g = jnp.pad(b_gates[l], (0, LANES - n_gate)).reshape(1, LANES)
    pml_l, pda_l, gates_l = _proj(x, sh1, sc1, w_main, w_g, b_g, _rope_tables(n), _pick(n, (512, 256, 128)))
    pml_c, pda_c, gates_c = _proj(ctx, csh1, csc1, w_main, w_g, b_g, None, _pick(n_ctx, (512, 256, 128)))

    qkc_l = _conv(pml_l, ml_conv_w[l], ml_conv_b[l], _pick(n, (512, 256, 128)))
    qkc_c = _conv(pml_c, ml_conv_w[l], ml_conv_b[l], _pick(n_ctx, (512, 256, 128)))
    gt_l = jnp.swapaxes(gates_l[:, :, :n_gate], 1, 2)
    gt_c = jnp.swapaxes(gates_c[:, :, :n_gate], 1, 2)
    kt_l = jnp.swapaxes(qkc_l[:, :, SEC // 2:], 1, 2)
    kt_c = jnp.swapaxes(qkc_c[:, :, SEC // 2:], 1, 2)
    c0 = jnp.zeros((bsz, 2, N_HEADS, ML_DQK, 2 * ML_DV), F32)
    m0 = jnp.zeros((bsz, 2, N_HEADS, 1, LANES), F32)
    _, _, c1, m1 = _mlstm(qkc_c, kt_c, pml_c, gates_c, gt_c, c0, m0)
    hf, hb, _, _ = _mlstm(qkc_l, kt_l, pml_l, gates_l, gt_l, c1, m1)

    m_all = n + n_ctx
    k_all = jnp.concatenate([pda_l[:, :, SEC:2 * SEC], pda_c[:, :, SEC:2 * SEC]], axis=1)
    v_all = jnp.concatenate([pda_l[:, :, 2 * SEC:], pda_c[:, :, 2 * SEC:]], axis=1)
    qt = pda_l[:, :, :SEC].reshape(bsz, n, N_HEADS, 2 * DA_DQK).transpose(0, 2, 3, 1)
    vt = v_all.reshape(bsz, m_all, N_HEADS, DA_DV).transpose(0, 2, 3, 1)
    ones_rows = jnp.zeros((bsz, N_HEADS, DA_VROWS - DA_DV, m_all), BF16).at[:, :, 0].set(1.0)
    vt = jnp.concatenate([vt, ones_rows], axis=2)
    od = _attn(da_lambda[l], qt, k_all, vt, _pick(n, (256, 128)), _pick(m_all, (1280, 256, 128)), lambda_init)

    w_router_pad = jnp.pad(w_router[l], ((0, 0), (0, LANES - n_experts)))
    h1, u, aff = _merge(hf, hb, pml_l, od, x, w_out[l].astype(BF16), ml_norm_w[l], da_norm_w[l], g1,
                        ln1_w[l], ln1_b[l], sh2, sc2, w_router_pad, _pick(n, (256, 128)), lambda_init, n_experts)

    aff_t = jnp.swapaxes(aff[:, :, :n_experts], 1, 2).reshape(bsz * n_experts, n // LANES, LANES)
    idx, gsel = _router(aff_t, cap)
    y = _moe(idx.reshape(bsz * n_experts, 1, cap), gsel, u.reshape(bsz * n, d), w_gate[l], w_up[l], w_down[l],
             jnp.zeros((bsz * n, d), F32), n, _pick(w_gate.shape[3], (256, 128)))
    return _final(h1, y.reshape(bsz, n, d), g2, ln2_w[l], ln2_b[l], _pick(n, (512, 256, 128)))
```

```python
import functools
import math

import jax
import jax.numpy as jnp
from jax import lax
from jax.experimental import pallas as pl
from jax.experimental.pallas import tpu as pltpu

F32 = jnp.float32
BF16 = jnp.bfloat16
I32 = jnp.int32

LANES = 128
SUBLANES = 8
GRID_W = 64
N_HEADS = 8
ML_DQK = 64
ML_DV = 128
ML_CONV_W = 5
ML_CHUNK = 128
DA_DQK = 64
DA_DV = 128
SEC = 1024
ROPE_BASE = 10000.0
LN_EPS = 1e-5
DEPTH = 1
DEEPNORM_ALPHA = (2.0 * DEPTH) ** 0.25
EC_CAPACITY_FACTOR = 2
VMEM_LIMIT = 56 * 1024 * 1024
DA_QSCALE = DA_DQK ** -0.5 * math.log2(math.e)
DMA_UNROLL = 8
DA_VROWS = DA_DV + 16


def _sigmoid(x):
    return 1.0 / (1.0 + jnp.exp(-x))


def _log_sigmoid(x):
    return jnp.minimum(x, 0.0) - jnp.log(1.0 + jnp.exp(-jnp.abs(x)))


def _ln_rows(x):
    mu = jnp.mean(x, axis=-1, keepdims=True)
    xc = x - mu
    var = jnp.mean(xc * xc, axis=-1, keepdims=True)
    return xc * lax.rsqrt(var + LN_EPS)


def _split3(x):
    hi = x.astype(BF16)
    r1 = x - hi.astype(F32)
    mid = r1.astype(BF16)
    lo = (r1 - mid.astype(F32)).astype(BF16)
    return hi, mid, lo


def _dot(a, b):
    return jnp.dot(a, b, preferred_element_type=F32)


def _dot_exact_lhs(a_bf16, x_f32):
    hi, mid, lo = _split3(x_f32)
    return _dot(a_bf16, hi) + _dot(a_bf16, mid) + _dot(a_bf16, lo)


def _dot_exact_rhs(x_f32, a_bf16):
    hi, mid, lo = _split3(x_f32)
    return _dot(hi, a_bf16) + _dot(mid, a_bf16) + _dot(lo, a_bf16)


def _ada_body(c_ref, w_ref, b_ref, o_ref):
    c = c_ref[...]
    a = c * _sigmoid(c)
    hi, mid, lo = _split3(a)
    w = w_ref[...]
    whi, wmid, wlo = _split3(w)
    acc = _dot(hi, whi) + _dot(hi, wmid) + _dot(mid, whi)
    acc = acc + _dot(mid, wmid) + _dot(hi, wlo) + _dot(lo, whi)
    o_ref[...] = acc + b_ref[...]


def _ada(cin, w, b):
    rows, d = cin.shape
    n = w.shape[1]
    tn = _pick(n, (1024, 512, 256, 128))
    return pl.pallas_call(
        _ada_body,
        grid=(n // tn,),
        in_specs=[pl.BlockSpec((rows, d), lambda j: (0, 0)),
                  pl.BlockSpec((d, tn), lambda j: (0, j)),
                  pl.BlockSpec((1, tn), lambda j: (0, j))],
        out_specs=pl.BlockSpec((rows, tn), lambda j: (0, j)),
        out_shape=jax.ShapeDtypeStruct((rows, n), F32),
        compiler_params=pltpu.CompilerParams(dimension_semantics=("arbitrary",),
                                             vmem_limit_bytes=VMEM_LIMIT),
        name="ada_mod",
    )(cin, w, b.reshape(1, n))


def _rope_tile(acc, cosf, sins):
    outs = []
    lane = lax.broadcasted_iota(I32, (acc.shape[0], LANES), 1)
    first = (lane & (DA_DQK - 1)) < (DA_DQK // 2)
    for t in range(acc.shape[1] // LANES):
        a = acc[:, t * LANES:(t + 1) * LANES]
        partner = jnp.where(first, pltpu.roll(a, LANES - DA_DQK // 2, 1), pltpu.roll(a, DA_DQK // 2, 1))
        outs.append(a * cosf + partner * sins)
    return jnp.concatenate(outs, axis=1)


def _proj_body(*refs, rope):
    if rope:
        (x_ref, sh_ref, sc_ref, w_ref, wg_ref, bg_ref, cos_ref, sin_ref,
         pml_ref, pda_ref, g_ref, xn_ref) = refs
    else:
        (x_ref, sh_ref, sc_ref, w_ref, wg_ref, bg_ref, pml_ref, pda_ref, g_ref, xn_ref) = refs
    j = pl.program_id(2)

    @pl.when(j == 0)
    def _():
        y = _ln_rows(x_ref[0]) * (1.0 + sc_ref[0]) + sh_ref[0]
        yb = y.astype(BF16)
        xn_ref[...] = yb
        g_ref[0] = _dot(yb, wg_ref[...]) + bg_ref[...]

    acc = _dot(xn_ref[...], w_ref[...])

    @pl.when(j < 3)
    def _():
        pml_ref[0] = acc

    @pl.when(j == 3)
    def _():
        q = _rope_tile(acc, cos_ref[...], sin_ref[...]) if rope else acc
        pda_ref[0] = (q * DA_QSCALE).astype(BF16)

    @pl.when(j == 4)
    def _():
        k = _rope_tile(acc, cos_ref[...], sin_ref[...]) if rope else acc
        pda_ref[0] = k.astype(BF16)

    @pl.when(j == 5)
    def _():
        pda_ref[0] = acc.astype(BF16)


def _proj(x, sh, sc, w_main, w_g, b_g, rope_tabs, tm):
    bsz, n, d = x.shape
    per_batch = sh.shape[0] == bsz
    rope = rope_tabs is not None
    mod_map = (lambda b, i, j: (b, 0, 0)) if per_batch else (lambda b, i, j: (0, 0, 0))
    in_specs = [pl.BlockSpec((1, tm, d), lambda b, i, j: (b, i, 0)),
                pl.BlockSpec((1, 1, d), mod_map),
                pl.BlockSpec((1, 1, d), mod_map),
                pl.BlockSpec((d, SEC), lambda b, i, j: (0, j)),
                pl.BlockSpec((d, LANES), lambda b, i, j: (0, 0)),
                pl.BlockSpec((1, LANES), lambda b, i, j: (0, 0))]
    args = [x, sh, sc, w_main, w_g, b_g]
    if rope:
        in_specs += [pl.BlockSpec((tm, LANES), lambda b, i, j: (i, 0)),
                     pl.BlockSpec((tm, LANES), lambda b, i, j: (i, 0))]
        args += list(rope_tabs)
    return pl.pallas_call(
        functools.partial(_proj_body, rope=rope),
        grid=(bsz, n // tm, 6),
        in_specs=in_specs,
        out_specs=[pl.BlockSpec((1, tm, SEC), lambda b, i, j: (b, i, jnp.minimum(j, 2))),
                   pl.BlockSpec((1, tm, SEC), lambda b, i, j: (b, i, jnp.maximum(j - 3, 0))),
                   pl.BlockSpec((1, tm, LANES), lambda b, i, j: (b, i, 0))],
        out_shape=[jax.ShapeDtypeStruct((bsz, n, 3 * SEC), F32),
                   jax.ShapeDtypeStruct((bsz, n, 3 * SEC), BF16),
                   jax.ShapeDtypeStruct((bsz, n, LANES), F32)],
        scratch_shapes=[pltpu.VMEM((tm, d), BF16)],
        compiler_params=pltpu.CompilerParams(dimension_semantics=("parallel", "parallel", "arbitrary"),
                                             vmem_limit_bytes=VMEM_LIMIT),
        name="in_proj_rope" if rope else "in_proj",
    )(*args)


def _conv_body(prev_ref, cur_ref, next_ref, w_ref, b_ref, s_ref, o_ref):
    i = pl.program_id(1)
    last = pl.num_programs(1) - 1
    cur = cur_ref[0]
    t = cur.shape[0]
    prev = jnp.where(i > 0, prev_ref[0], 0.0)
    nxt = jnp.where(i < last, next_ref[0], 0.0)
    ext = jnp.concatenate([prev, cur, nxt], axis=0)
    acc = jnp.zeros_like(cur) + b_ref[...]
    half = ML_CONV_W // 2
    for k in range(ML_CONV_W):
        off = SUBLANES - half + k
        acc = acc + ext[off:off + t] * w_ref[k:k + 1, :]
    y = acc * _sigmoid(acc)
    o_ref[0] = (y * s_ref[...]).astype(BF16)


def _conv(p_ml, conv_w, conv_b, tt):
    bsz, n, _ = p_ml.shape
    nb8 = n // SUBLANES
    r = tt // SUBLANES
    scale = jnp.concatenate([jnp.full((1, SEC // 2), ML_DQK ** -0.5, F32), jnp.ones((1, SEC // 2), F32)], axis=1)
    return pl.pallas_call(
        _conv_body,
        grid=(bsz, n // tt),
        in_specs=[pl.BlockSpec((1, SUBLANES, SEC), lambda b, i: (b, jnp.maximum(i * r - 1, 0), 0)),
                  pl.BlockSpec((1, tt, SEC), lambda b, i: (b, i, 0)),
                  pl.BlockSpec((1, SUBLANES, SEC), lambda b, i: (b, jnp.minimum((i + 1) * r, nb8 - 1), 0)),
                  pl.BlockSpec((ML_CONV_W, SEC), lambda b, i: (0, 0)),
                  pl.BlockSpec((1, SEC), lambda b, i: (0, 0)),
                  pl.BlockSpec((1, SEC), lambda b, i: (0, 0))],
        out_specs=pl.BlockSpec((1, tt, SEC), lambda b, i: (b, i, 0)),
        out_shape=jax.ShapeDtypeStruct((bsz, n, SEC), BF16),
        compiler_params=pltpu.CompilerParams(dimension_semantics=("parallel", "parallel"),
                                             vmem_limit_bytes=VMEM_LIMIT),
        name="ml_conv",
    )(p_ml, p_ml, p_ml, conv_w, conv_b.reshape(1, SEC), scale)


def _mlstm_head(q, kt, v_ext, b_col, i_row, b_row, b_tot, mask, c_prev, m_prev):
    dmat = jnp.where(mask, b_col + (i_row - b_row), -jnp.inf)
    m_inter = b_col + m_prev
    m_t = jnp.maximum(m_inter, jnp.max(dmat, axis=-1, keepdims=True))
    w_inter = jnp.exp(m_inter - m_t)
    w = jnp.exp(dmat - m_t) * _dot(q, kt)
    ext = w_inter * _dot(q, c_prev.astype(BF16)) + _dot(w.astype(BF16), v_ext)
    h = ext[:, :ML_DV] / jnp.maximum(jnp.abs(ext[:, ML_DV:ML_DV + 1]), jnp.exp(-m_t))
    gdec_row = b_tot - b_row + i_row
    m_new = jnp.maximum(b_tot + m_prev, jnp.max(gdec_row, axis=-1, keepdims=True))
    a = jnp.exp(b_tot + m_prev - m_new)
    kwt = (kt.astype(F32) * jnp.exp(gdec_row - m_new)).astype(BF16)
    c_new = a * c_prev + _dot(kwt, v_ext)
    return h, c_new, m_new


def _mlstm_body(qf_ref, ktf_ref, vf_ref, gf_ref, gtf_ref, qb_ref, ktb_ref, vb_ref, gb_ref, gtb_ref,
                c0_ref, m0_ref, hf_ref, hb_ref, c_ref, m_ref):
    @pl.when(pl.program_id(1) == 0)
    def _():
        c_ref[...] = c0_ref[...]
        m_ref[...] = m0_ref[...]

    L = ML_CHUNK
    r = lax.broadcasted_iota(I32, (L, L), 0)
    s = lax.broadcasted_iota(I32, (L, L), 1)
    mle = r <= s
    mge = r >= s
    mle_b = mle.astype(BF16)
    mge_b = mge.astype(BF16)
    ones_blk = (lax.broadcasted_iota(I32, (L, ML_DV), 1) == 0).astype(BF16)
    dirs = ((qf_ref, ktf_ref, vf_ref, gf_ref, gtf_ref, hf_ref), (qb_ref, ktb_ref, vb_ref, gb_ref, gtb_ref, hb_ref))
    for d, (q_ref, kt_ref, v_ref, g_ref, gt_ref, h_ref) in enumerate(dirs):
        g = g_ref[0]
        gt = gt_ref[0]
        bc = _dot_exact_lhs(mge_b if d == 0 else mle_b, _log_sigmoid(g))
        br = _dot_exact_rhs(_log_sigmoid(gt), mle_b if d == 0 else mge_b)
        mask = mge if d == 0 else mle
        qa = q_ref[0]
        kta = kt_ref[0]
        va = v_ref[0].astype(BF16)
        outs = []
        for h in range(N_HEADS):
            ci = 2 * N_HEADS * d + h
            cf = ci + N_HEADS
            b_row = br[cf:cf + 1, :]
            b_tot = b_row[:, L - 1:L] if d == 0 else b_row[:, 0:1]
            v_ext = jnp.concatenate([va[:, h * ML_DV:(h + 1) * ML_DV], ones_blk], axis=1)
            outs.append(_mlstm_head(
                qa[:, h * ML_DQK:(h + 1) * ML_DQK], kta[h * ML_DQK:(h + 1) * ML_DQK, :], v_ext,
                bc[:, cf:cf + 1], gt[ci:ci + 1, :], b_row, b_tot, mask,
                c_ref[0, d, h], m_ref[0, d, h][:, 0:1]))
        h_ref[0] = jnp.concatenate([o[0] for o in outs], axis=1)
        c_ref[0, d] = jnp.stack([o[1] for o in outs])
        m_ref[0, d] = jnp.stack([jnp.broadcast_to(o[2], (1, LANES)) for o in outs])


def _mlstm(qkc, kt, p_ml, gates, gates_t, c0, m0):
    bsz, n, _ = qkc.shape
    L = ML_CHUNK
    nc = n // L
    hq = SEC // 2
    fwd = lambda b, i: (b, i, 0)
    bwd = lambda b, i: (b, nc - 1 - i, 0)
    st = lambda b, i: (b, 0, 0, 0, 0)
    c_spec = pl.BlockSpec((1, 2, N_HEADS, ML_DQK, 2 * ML_DV), st)
    m_spec = pl.BlockSpec((1, 2, N_HEADS, 1, LANES), st)

    def dir_specs(pos):
        return [pl.BlockSpec((1, L, hq), lambda b, i: (b, pos(i), 0)),
                pl.BlockSpec((1, hq, L), lambda b, i: (b, 0, pos(i))),
                pl.BlockSpec((1, L, SEC), lambda b, i: (b, pos(i), 1)),
                pl.BlockSpec((1, L, LANES), lambda b, i: (b, pos(i), 0)),
                pl.BlockSpec((1, 4 * N_HEADS, L), lambda b, i: (b, 0, pos(i)))]

    return pl.pallas_call(
        _mlstm_body,
        grid=(bsz, nc),
        in_specs=dir_specs(lambda i: i) + dir_specs(lambda i: nc - 1 - i) + [c_spec, m_spec],
        out_specs=[pl.BlockSpec((1, L, SEC), fwd), pl.BlockSpec((1, L, SEC), bwd), c_spec, m_spec],
        out_shape=[jax.ShapeDtypeStruct((bsz, n, SEC), F32), jax.ShapeDtypeStruct((bsz, n, SEC), F32),
                   jax.ShapeDtypeStruct(c0.shape, F32), jax.ShapeDtypeStruct(m0.shape, F32)],
        compiler_params=pltpu.CompilerParams(dimension_semantics=("parallel", "arbitrary"),
                                             vmem_limit_bytes=VMEM_LIMIT),
        name="mlstm",
    )(qkc, kt, p_ml, gates, gates_t, qkc, kt, p_ml, gates, gates_t, c0, m0)


def _attn_body(lam_ref, q_ref, k_ref, vt_ref, o_ref, qq_s, sa, sb, ma, mb, m_s, acc_s, *, tq, tk, lambda_init):
    nk = k_ref.shape[1] // tk
    nq = q_ref.shape[1] // tq
    blocks_per_iter = 1 if nk % 2 == 0 else 2
    assert nq % blocks_per_iter == 0
    bufs = ((sa, ma), (sb, mb))
    lm = lam_ref[...]
    lam = (jnp.exp(jnp.sum(lm[0:1] * lm[1:2], axis=-1, keepdims=True))
           - jnp.exp(jnp.sum(lm[2:3] * lm[3:4], axis=-1, keepdims=True)) + lambda_init)

    def scores(blk, t, buf):
        s_ref, mt_ref = buf
        if t == 0:
            qoff = pl.multiple_of(jnp.minimum(blk, nq - 1) * tq, LANES)
            qt = jnp.transpose(q_ref[0, pl.ds(qoff, tq), :])
            row = lax.broadcasted_iota(I32, qt.shape, 0)
            zero = jnp.zeros_like(qt)
            qq_s[...] = jnp.concatenate([jnp.where(row < DA_DQK, qt, zero), jnp.where(row >= DA_DQK, qt, zero)],
                                        axis=1)
        s = _dot(k_ref[0, t * tk:(t + 1) * tk, :], qq_s[...])
        s_ref[...] = s
        mt_ref[...] = jnp.max(s.reshape(tk // SUBLANES, SUBLANES, 2 * tq).max(axis=0), axis=0, keepdims=True)

    def absorb(blk, t, buf):
        s_ref, mt_ref = buf
        vts = vt_ref[0, 0, :, t * tk:(t + 1) * tk]
        if t == 0:
            m_new = mt_ref[...]
            acc = _dot(vts, jnp.exp2(s_ref[...] - m_new).astype(BF16))
        else:
            m_old = m_s[...]
            m_new = jnp.maximum(m_old, mt_ref[...])
            p = jnp.exp2(s_ref[...] - m_new)
            acc = jnp.exp2(m_old - m_new) * acc_s[...] + _dot(vts, p.astype(BF16))
        if t == nk - 1:
            o2 = acc[:DA_DV] / acc[DA_DV:DA_DV + 1]
            qoff = pl.multiple_of(blk * tq, LANES)
            o_ref[0, pl.ds(qoff, tq), :] = jnp.transpose(o2[:, :tq] - lam * o2[:, tq:])
        else:
            acc_s[...] = acc
            m_s[...] = m_new

    scores(0, 0, bufs[0])

    def step(j, carry):
        for b in range(blocks_per_iter):
            blk = j * blocks_per_iter + b
            for t in range(nk):
                cur = (b * nk + t) % 2
                if t + 1 < nk:
                    scores(blk, t + 1, bufs[1 - cur])
                else:
                    scores(blk + 1, 0, bufs[1 - cur])
                absorb(blk, t, bufs[cur])
        return carry

    lax.fori_loop(0, nq // blocks_per_iter, step, 0)


def _attn(da_lambda, p_da, k_all, vt, tq, tk, lambda_init):
    bsz, n, _ = p_da.shape
    hd = 2 * DA_DQK
    m = k_all.shape[1]
    return pl.pallas_call(
        functools.partial(_attn_body, tq=tq, tk=tk, lambda_init=lambda_init),
        grid=(bsz, N_HEADS),
        in_specs=[pl.BlockSpec((4, DA_DQK), lambda b, h: (0, 0)),
                  pl.BlockSpec((1, n, hd), lambda b, h: (b, 0, h)),
                  pl.BlockSpec((1, m, hd), lambda b, h: (b, 0, h)),
                  pl.BlockSpec((1, 1, DA_VROWS, m), lambda b, h: (b, h, 0, 0))],
        out_specs=pl.BlockSpec((1, n, DA_DV), lambda b, h: (b, 0, h)),
        out_shape=jax.ShapeDtypeStruct((bsz, n, SEC), F32),
        scratch_shapes=[pltpu.VMEM((hd, 2 * tq), BF16),
                        pltpu.VMEM((tk, 2 * tq), F32), pltpu.VMEM((tk, 2 * tq), F32),
                        pltpu.VMEM((1, 2 * tq), F32), pltpu.VMEM((1, 2 * tq), F32),
                        pltpu.VMEM((1, 2 * tq), F32), pltpu.VMEM((DA_VROWS, 2 * tq), F32)],
        compiler_params=pltpu.CompilerParams(dimension_semantics=("parallel", "parallel"),
                                             vmem_limit_bytes=VMEM_LIMIT),
        name="diff_attn",
    )(da_lambda, p_da, k_all, vt)


def _merge_body(hf_ref, hb_ref, o_ref, od_ref, x_ref, wout_ref, mlw_ref, daw_ref, g1_ref, ln1w_ref, ln1b_ref,
                sh2_ref, sc2_ref, wr_ref, h1_ref, u_ref, aff_ref, *, lambda_init, n_experts):
    hsum = hf_ref[0] + hb_ref[0]
    og = o_ref[0]
    od = od_ref[0]
    mlw = mlw_ref[...]
    daw = daw_ref[...]
    parts = []
    for h in range(N_HEADS):
        sl = slice(h * ML_DV, (h + 1) * ML_DV)
        parts.append(_ln_rows(hsum[:, sl]) * mlw[:, sl] * _sigmoid(og[:, sl]))
    for h in range(N_HEADS):
        sl = slice(h * DA_DV, (h + 1) * DA_DV)
        z = od[:, sl]
        zn = z * lax.rsqrt(jnp.mean(z * z, axis=-1, keepdims=True) + LN_EPS)
        parts.append(zn * daw[:, sl] * (1.0 - lambda_init))
    ycat = jnp.concatenate(parts, axis=1).astype(BF16)
    mix = _dot(ycat, wout_ref[...])
    h1 = _ln_rows(DEEPNORM_ALPHA * x_ref[0] + g1_ref[0] * mix) * ln1w_ref[...] + ln1b_ref[...]
    h1_ref[0] = h1
    u = _ln_rows(h1) * (1.0 + sc2_ref[0]) + sh2_ref[0]
    u_ref[0] = u
    uhi, umid, _ = _split3(u)
    whi, wmid, _ = _split3(wr_ref[...])
    logits = _dot(uhi, whi) + _dot(uhi, wmid) + _dot(umid, whi)
    lane = lax.broadcasted_iota(I32, logits.shape, 1)
    logits = jnp.where(lane < n_experts, logits, -jnp.inf)
    e = jnp.exp(logits - jnp.max(logits, axis=-1, keepdims=True))
    aff_ref[0] = e / jnp.sum(e, axis=-1, keepdims=True)


def _merge(hf, hb, p_ml, od, x, w_out, ml_norm_w, da_norm_w, g1, ln1_w, ln1_b, sh2, sc2, w_router_pad,
           tm, lambda_init, n_experts):
    bsz, n, d = x.shape
    row = lambda b, i: (b, i, 0)
    vec = lambda b, i: (0, 0)
    mod = lambda b, i: (b, 0, 0)
    return pl.pallas_call(
        functools.partial(_merge_body, lambda_init=lambda_init, n_experts=n_experts),
        grid=(bsz, n // tm),
        in_specs=[pl.BlockSpec((1, tm, SEC), row), pl.BlockSpec((1, tm, SEC), row),
                  pl.BlockSpec((1, tm, SEC), lambda b, i: (b, i, 2)),
                  pl.BlockSpec((1, tm, SEC), row),
                  pl.BlockSpec((1, tm, d), row),
                  pl.BlockSpec((2 * SEC, d), vec),
                  pl.BlockSpec((1, SEC), vec), pl.BlockSpec((1, SEC), vec),
                  pl.BlockSpec((1, 1, d), mod),
                  pl.BlockSpec((1, d), vec), pl.BlockSpec((1, d), vec),
                  pl.BlockSpec((1, 1, d), mod), pl.BlockSpec((1, 1, d), mod),
                  pl.BlockSpec((d, LANES), vec)],
        out_specs=[pl.BlockSpec((1, tm, d), row), pl.BlockSpec((1, tm, d), row),
                   pl.BlockSpec((1, tm, LANES), row)],
        out_shape=[jax.ShapeDtypeStruct((bsz, n, d), F32), jax.ShapeDtypeStruct((bsz, n, d), F32),
                   jax.ShapeDtypeStruct((bsz, n, LANES), F32)],
        compiler_params=pltpu.CompilerParams(dimension_semantics=("parallel", "parallel"),
                                             vmem_limit_bytes=VMEM_LIMIT),
        name="merge_ln1_router",
    )(hf, hb, p_ml, od, x, w_out, ml_norm_w.reshape(1, SEC), da_norm_w.reshape(1, SEC), g1,
      ln1_w.reshape(1, d), ln1_b.reshape(1, d), sh2, sc2, w_router_pad)


def _cumsum_tokens(maskf, uincl_b, lstrict_b):
    mb = maskf.astype(BF16)
    win = _dot(mb, uincl_b)
    tot = jnp.broadcast_to(win[:, LANES - 1:LANES], win.shape)
    offs = _dot(lstrict_b, tot.astype(BF16))
    return offs + win, tot


def _router_body(a_ref, idx_ref, g_ref, *, cap):
    a = a_ref[0]
    rr = a.shape[0]
    bits = lax.bitcast_convert_type(a, I32)
    cur = jnp.zeros((1, 1), I32)
    for bit in range(30, -1, -1):
        cand = cur | (1 << bit)
        cnt = jnp.sum((bits >= cand).astype(I32), keepdims=True)
        cur = jnp.where(cnt >= cap, cand, cur)
    gt = bits > cur
    eq = bits == cur
    need = (cap - jnp.sum(gt.astype(I32), keepdims=True)).astype(F32)

    li = lax.broadcasted_iota(I32, (LANES, LANES), 0)
    lj = lax.broadcasted_iota(I32, (LANES, LANES), 1)
    uincl_b = (li <= lj).astype(BF16)
    ri = lax.broadcasted_iota(I32, (rr, rr), 0)
    rj = lax.broadcasted_iota(I32, (rr, rr), 1)
    lstrict_b = (rj < ri).astype(BF16)
    rincl_b = (ri <= rj).astype(BF16)

    eqf = eq.astype(F32)
    eq_incl, _ = _cumsum_tokens(eqf, uincl_b, lstrict_b)
    sel = gt | (eq & ((eq_incl - eqf) < need))
    self_ = sel.astype(F32)
    csum, _ = _cumsum_tokens(self_, uincl_b, lstrict_b)

    ones8 = jnp.ones((SUBLANES, LANES), BF16)
    tot_lane = lax.dot_general(ones8, self_.astype(BF16), (((1,), (1,)), ((), ())),
                               preferred_element_type=F32)
    end_lane = _dot(tot_lane.astype(BF16), rincl_b)[0:1]
    beg_lane = end_lane - tot_lane[0:1]
    j = lax.broadcasted_iota(I32, (cap, 1), 0).astype(F32)
    onehot = ((beg_lane <= j) & (end_lane > j)).astype(BF16)
    row_idx = jnp.sum((end_lane <= j).astype(F32), axis=-1, keepdims=True)
    chi = jnp.floor(csum * (1.0 / LANES))
    clo = csum - chi * LANES
    crow = _dot(onehot, chi.astype(BF16)) * LANES + _dot(onehot, clo.astype(BF16))
    lane_idx = jnp.sum((crow <= j).astype(F32), axis=-1, keepdims=True)
    idx_ref[0] = (row_idx * LANES + lane_idx).astype(I32)
    ahi, amid, alo = _split3(a)
    arow = _dot(onehot, ahi) + _dot(onehot, amid) + _dot(onehot, alo)
    lane = lax.broadcasted_iota(I32, (cap, LANES), 1).astype(F32)
    g_ref[0] = jnp.sum(jnp.where(lane == lane_idx, arow, 0.0), axis=-1, keepdims=True)


def _router(aff_t, cap):
    g, rr, _ = aff_t.shape
    return pl.pallas_call(
        functools.partial(_router_body, cap=cap),
        grid=(g,),
        in_specs=[pl.BlockSpec((1, rr, LANES), lambda i: (i, 0, 0))],
        out_specs=[pl.BlockSpec((1, cap, 1), lambda i: (i, 0, 0)), pl.BlockSpec((1, cap, 1), lambda i: (i, 0, 0))],
        out_shape=[jax.ShapeDtypeStruct((g, cap, 1), I32), jax.ShapeDtypeStruct((g, cap, 1), F32)],
        compiler_params=pltpu.CompilerParams(dimension_semantics=("parallel",), vmem_limit_bytes=VMEM_LIMIT),
        name="ec_select",
    )(aff_t)


def _moe_body(idx_ref, g_ref, u_hbm, wg_ref, wu_ref, wd_ref, y_in_hbm, y_hbm, buf, xs, tmp, sem,
              *, cap, n_tok, n_experts, rc):
    del y_in_hbm
    grp = pl.program_id(0)
    h = pl.program_id(1)
    base = (grp // n_experts) * n_tok

    @pl.when(h == 0)
    def _():
        def issue(j, carry):
            row = base + idx_ref[0, 0, j]
            pltpu.make_async_copy(u_hbm.at[pl.ds(row, 1)], buf.at[pl.ds(j, 1)], sem.at[0]).start()
            return carry
        lax.fori_loop(0, cap, issue, 0, unroll=DMA_UNROLL)
        pltpu.make_async_copy(u_hbm.at[pl.ds(0, cap)], buf, sem.at[0]).wait()
        xs[...] = buf[...].astype(BF16)
        buf[...] = jnp.zeros(buf.shape, F32)

    x = xs[...]
    a = _dot(x, wg_ref[0].astype(BF16))
    b = _dot(x, wu_ref[0].astype(BF16))
    hid = (a * _sigmoid(a) * b).astype(BF16)
    buf[...] += _dot(hid, wd_ref[0].astype(BF16))

    def y_rows(c, slot, to_hbm):
        def issue(r, carry):
            row = base + idx_ref[0, 0, c * rc + r]
            hbm_row = y_hbm.at[pl.ds(row, 1)]
            vmem_row = tmp.at[slot, pl.ds(r, 1)]
            if to_hbm:
                pltpu.make_async_copy(vmem_row, hbm_row, sem.at[3 + slot]).start()
            else:
                pltpu.make_async_copy(hbm_row, vmem_row, sem.at[1 + slot]).start()
            return carry
        lax.fori_loop(0, rc, issue, 0, unroll=DMA_UNROLL)

    def y_wait(slot, to_hbm):
        hbm_rows = y_hbm.at[pl.ds(0, rc)]
        if to_hbm:
            pltpu.make_async_copy(tmp.at[slot], hbm_rows, sem.at[3 + slot]).wait()
        else:
            pltpu.make_async_copy(hbm_rows, tmp.at[slot], sem.at[1 + slot]).wait()

    @pl.when(h == pl.num_programs(1) - 1)
    def _():
        n_chunks = cap // rc
        y_rows(0, 0, False)
        for c in range(n_chunks):
            slot = c % 2
            if c + 1 < n_chunks:
                if c >= 1:
                    y_wait(1 - slot, True)
                y_rows(c + 1, 1 - slot, False)
            y_wait(slot, False)
            tmp[slot] = tmp[slot] + g_ref[0, c * rc:(c + 1) * rc, :] * buf[c * rc:(c + 1) * rc, :]
            y_rows(c, slot, True)
        for c in range(max(n_chunks - 2, 0), n_chunks):
            y_wait(c % 2, True)


def _moe(idx3, gsel, u2, w_gate, w_up, w_down, y0, n_tok, th):
    g, _, cap = idx3.shape
    n_experts, d, de = w_gate.shape
    rc = min(256, max(cap // 4, SUBLANES))
    return pl.pallas_call(
        functools.partial(_moe_body, cap=cap, n_tok=n_tok, n_experts=n_experts, rc=rc),
        grid=(g, de // th),
        in_specs=[pl.BlockSpec((1, 1, cap), lambda i, h: (i, 0, 0), memory_space=pltpu.SMEM),
                  pl.BlockSpec((1, cap, 1), lambda i, h: (i, 0, 0)),
                  pl.BlockSpec(memory_space=pl.ANY),
                  pl.BlockSpec((1, d, th), lambda i, h: (i % n_experts, 0, h)),
                  pl.BlockSpec((1, d, th), lambda i, h: (i % n_experts, 0, h)),
                  pl.BlockSpec((1, th, d), lambda i, h: (i % n_experts, h, 0)),
                  pl.BlockSpec(memory_space=pl.ANY)],
        out_specs=pl.BlockSpec(memory_space=pl.ANY),
        out_shape=jax.ShapeDtypeStruct(y0.shape, F32),
        scratch_shapes=[pltpu.VMEM((cap, d), F32), pltpu.VMEM((cap, d), BF16), pltpu.VMEM((2, rc, d), F32),
                        pltpu.SemaphoreType.DMA((5,))],
        input_output_aliases={6: 0},
        compiler_params=pltpu.CompilerParams(dimension_semantics=("arbitrary", "arbitrary"),
                                             vmem_limit_bytes=VMEM_LIMIT, disable_bounds_checks=True),
        name="ec_moe",
    )(idx3, gsel, u2, w_gate, w_up, w_down, y0)


def _final_body(h1_ref, y_ref, g2_ref, w_ref, b_ref, o_ref):
    t = DEEPNORM_ALPHA * h1_ref[0] + g2_ref[0] * y_ref[0]
    o_ref[0] = _ln_rows(t) * w_ref[...] + b_ref[...]


def _final(h1, y, g2, ln2_w, ln2_b, tm):
    bsz, n, d = h1.shape
    row = lambda b, i: (b, i, 0)
    return pl.pallas_call(
        _final_body,
        grid=(bsz, n // tm),
        in_specs=[pl.BlockSpec((1, tm, d), row), pl.BlockSpec((1, tm, d), row),
                  pl.BlockSpec((1, 1, d), lambda b, i: (b, 0, 0)),
                  pl.BlockSpec((1, d), lambda b, i: (0, 0)), pl.BlockSpec((1, d), lambda b, i: (0, 0))],
        out_specs=pl.BlockSpec((1, tm, d), row),
        out_shape=jax.ShapeDtypeStruct((bsz, n, d), F32),
        compiler_params=pltpu.CompilerParams(dimension_semantics=("parallel", "parallel"),
                                             vmem_limit_bytes=VMEM_LIMIT),
        name="ln2",
    )(h1, y, g2, ln2_w.reshape(1, d), ln2_b.reshape(1, d))


def _rope_tables(n):
    rows = n // GRID_W
    pos_row = jnp.repeat(jnp.arange(rows, dtype=I32), GRID_W).astype(F32)
    pos_col = jnp.tile(jnp.arange(GRID_W, dtype=I32), rows).astype(F32)
    n_freq = DA_DQK // 4
    inv = ROPE_BASE ** (-jnp.arange(n_freq, dtype=F32) / n_freq)
    ang = jnp.concatenate([pos_row[:, None] * inv, pos_col[:, None] * inv], -1)
    cos, sin = jnp.cos(ang), jnp.sin(ang)
    reps = LANES // DA_DQK
    cosf = jnp.tile(cos, (1, 2 * reps))
    sins = jnp.tile(jnp.concatenate([-sin, sin], -1), (1, reps))
    return cosf, sins


def _pick(n, prefs):
    for t in prefs:
        if n % t == 0:
            return t
    return n


def kernel(x, c, ctx, c_ctx, w_ada, b_ada, w_in, b_gates, ml_conv_w, ml_conv_b, ml_norm_w, da_lambda, da_norm_w,
           w_out, ln1_w, ln1_b, w_router, w_gate, w_up, w_down, ln2_w, ln2_b):
    bsz, n, d = x.shape
    n_ctx = ctx.shape[1]
    n_experts = w_gate.shape[1]
    cap = EC_CAPACITY_FACTOR * n // n_experts
    l = 0
    lambda_init = 0.8 - 0.6 * math.exp(-0.3 * l)

    cin = jnp.zeros((SUBLANES, d), F32).at[:bsz].set(c).at[bsz].set(c_ctx)
    mod = _ada(cin, w_ada[l], b_ada[l])
    sh1, sc1, g1, sh2, sc2, g2 = (mod[:bsz, k * d:(k + 1) * d][:, None, :] for k in range(6))
    csh1, csc1 = (mod[bsz:bsz + 1, k * d:(k + 1) * d][:, None, :] for k in range(2))

    w = w_in[l]
    off_gates = 3 * SEC
    n_gate = 4 * N_HEADS
    w_main = jnp.concatenate([w[:, :off_gates], w[:, off_gates + n_gate:]], axis=1).astype(BF16)
    w_g = jnp.pad(w[:, off_gates:off_gates + n_gate], ((0, 0), (0, LANES - n_gate))).astype(BF16)
    b_g = jnp.pad(b_gates[l], (0, LANES - n_gate)).reshape(1, LANES)
    pml_l, pda_l, gates_l = _proj(x, sh1, sc1, w_main, w_g, b_g, _rope_tables(n), _pick(n, (512, 256, 128)))
    pml_c, pda_c, gates_c = _proj(ctx, csh1, csc1, w_main, w_g, b_g, None, _pick(n_ctx, (512, 256, 128)))

    qkc_l = _conv(pml_l, ml_conv_w[l], ml_conv_b[l], _pick(n, (512, 256, 128)))
    qkc_c = _conv(pml_c, ml_conv_w[l], ml_conv_b[l], _pick(n_ctx, (512, 256, 128)))
    gt_l = jnp.swapaxes(gates_l[:, :, :n_gate], 1, 2)
    gt_c = jnp.swapaxes(gates_c[:, :, :n_gate], 1, 2)
    kt_l = jnp.swapaxes(qkc_l[:, :, SEC // 2:], 1, 2)
    kt_c = jnp.swapaxes(qkc_c[:, :, SEC // 2:], 1, 2)
    c0 = jnp.zeros((bsz, 2, N_HEADS, ML_DQK, 2 * ML_DV), F32)
    m0 = jnp.zeros((bsz, 2, N_HEADS, 1, LANES), F32)
    _, _, c1, m1 = _mlstm(qkc_c, kt_c, pml_c, gates_c, gt_c, c0, m0)
    hf, hb, _, _ = _mlstm(qkc_l, kt_l, pml_l, gates_l, gt_l, c1, m1)

    m_all = n + n_ctx
    k_all = jnp.concatenate([pda_l[:, :, SEC:2 * SEC], pda_c[:, :, SEC:2 * SEC]], axis=1)
    v_all = jnp.concatenate([pda_l[:, :, 2 * SEC:], pda_c[:, :, 2 * SEC:]], axis=1)
    vt = v_all.reshape(bsz, m_all, N_HEADS, DA_DV).transpose(0, 2, 3, 1)
    ones_rows = jnp.zeros((bsz, N_HEADS, DA_VROWS - DA_DV, m_all), BF16).at[:, :, 0].set(1.0)
    vt = jnp.concatenate([vt, ones_rows], axis=2)
    od = _attn(da_lambda[l], pda_l, k_all, vt, _pick(n, (256, 128)), _pick(m_all, (1280, 256, 128)), lambda_init)

    w_router_pad = jnp.pad(w_router[l], ((0, 0), (0, LANES - n_experts)))
    h1, u, aff = _merge(hf, hb, pml_l, od, x, w_out[l].astype(BF16), ml_norm_w[l], da_norm_w[l], g1,
                        ln1_w[l], ln1_b[l], sh2, sc2, w_router_pad, _pick(n, (256, 128)), lambda_init, n_experts)

    aff_t = jnp.swapaxes(aff[:, :, :n_experts], 1, 2).reshape(bsz * n_experts, n // LANES, LANES)
    idx, gsel = _router(aff_t, cap)
    y = _moe(idx.reshape(bsz * n_experts, 1, cap), gsel, u.reshape(bsz * n, d), w_gate[l], w_up[l], w_down[l],
             jnp.zeros((bsz * n, d), F32), n, _pick(w_gate.shape[3], (256, 128)))
    return _final(h1, y.reshape(bsz, n, d), g2, ln2_w[l], ln2_b[l], _pick(n, (512, 256, 128)))
```

```python
import functools
import math

import jax
import jax.numpy as jnp
from jax import lax
from jax.experimental import pallas as pl
from jax.experimental.pallas import tpu as pltpu

F32 = jnp.float32
BF16 = jnp.bfloat16
I32 = jnp.int32

LANES = 128
SUBLANES = 8
GRID_W = 64
N_HEADS = 8
ML_DQK = 64
ML_DV = 128
ML_CONV_W = 5
ML_CHUNK = 128
DA_DQK = 64
DA_DV = 128
SEC = 1024
ROPE_BASE = 10000.0
LN_EPS = 1e-5
DEPTH = 1
DEEPNORM_ALPHA = (2.0 * DEPTH) ** 0.25
EC_CAPACITY_FACTOR = 2
VMEM_LIMIT = 56 * 1024 * 1024
DA_QSCALE = DA_DQK ** -0.5 * math.log2(math.e)
DA_VROWS = DA_DV + 16


def _sigmoid(x):
    return 1.0 / (1.0 + jnp.exp(-x))


def _log_sigmoid(x):
    return jnp.minimum(x, 0.0) - jnp.log(1.0 + jnp.exp(-jnp.abs(x)))


def _ln_rows(x):
    mu = jnp.mean(x, axis=-1, keepdims=True)
    xc = x - mu
    var = jnp.mean(xc * xc, axis=-1, keepdims=True)
    return xc * lax.rsqrt(var + LN_EPS)


def _split3(x):
    hi = x.astype(BF16)
    r1 = x - hi.astype(F32)
    mid = r1.astype(BF16)
    lo = (r1 - mid.astype(F32)).astype(BF16)
    return hi, mid, lo


def _dot(a, b):
    return jnp.dot(a, b, preferred_element_type=F32)


def _dot_exact_lhs(a_bf16, x_f32):
    hi, mid, lo = _split3(x_f32)
    return _dot(a_bf16, hi) + _dot(a_bf16, mid) + _dot(a_bf16, lo)


def _dot_exact_rhs(x_f32, a_bf16):
    hi, mid, lo = _split3(x_f32)
    return _dot(hi, a_bf16) + _dot(mid, a_bf16) + _dot(lo, a_bf16)


def _ada_body(c_ref, w_ref, b_ref, o_ref):
    c = c_ref[...]
    a = c * _sigmoid(c)
    hi, mid, lo = _split3(a)
    w = w_ref[...]
    whi, wmid, wlo = _split3(w)
    acc = _dot(hi, whi) + _dot(hi, wmid) + _dot(mid, whi)
    acc = acc + _dot(mid, wmid) + _dot(hi, wlo) + _dot(lo, whi)
    o_ref[...] = acc + b_ref[...]


def _ada(cin, w, b):
    rows, d = cin.shape
    n = w.shape[1]
    tn = _pick(n, (1024, 512, 256, 128))
    return pl.pallas_call(
        _ada_body,
        grid=(n // tn,),
        in_specs=[pl.BlockSpec((rows, d), lambda j: (0, 0)),
                  pl.BlockSpec((d, tn), lambda j: (0, j)),
                  pl.BlockSpec((1, tn), lambda j: (0, j))],
        out_specs=pl.BlockSpec((rows, tn), lambda j: (0, j)),
        out_shape=jax.ShapeDtypeStruct((rows, n), F32),
        compiler_params=pltpu.CompilerParams(dimension_semantics=("arbitrary",),
                                             vmem_limit_bytes=VMEM_LIMIT),
        name="ada_mod",
    )(cin, w, b.reshape(1, n))


def _rope_tile(acc, cosf, sins):
    outs = []
    lane = lax.broadcasted_iota(I32, (acc.shape[0], LANES), 1)
    first = (lane & (DA_DQK - 1)) < (DA_DQK // 2)
    for t in range(acc.shape[1] // LANES):
        a = acc[:, t * LANES:(t + 1) * LANES]
        partner = jnp.where(first, pltpu.roll(a, LANES - DA_DQK // 2, 1), pltpu.roll(a, DA_DQK // 2, 1))
        outs.append(a * cosf + partner * sins)
    return jnp.concatenate(outs, axis=1)


def _proj_body(*refs, rope):
    if rope:
        (x_ref, sh_ref, sc_ref, w_ref, wg_ref, bg_ref, cos_ref, sin_ref,
         pml_ref, pda_ref, g_ref, xn_ref) = refs
    else:
        (x_ref, sh_ref, sc_ref, w_ref, wg_ref, bg_ref, pml_ref, pda_ref, g_ref, xn_ref) = refs
    j = pl.program_id(2)

    @pl.when(j == 0)
    def _():
        y = _ln_rows(x_ref[0]) * (1.0 + sc_ref[0]) + sh_ref[0]
        yb = y.astype(BF16)
        xn_ref[...] = yb
        g_ref[0] = _dot(yb, wg_ref[...]) + bg_ref[...]

    acc = _dot(xn_ref[...], w_ref[...])

    @pl.when(j < 3)
    def _():
        pml_ref[0] = acc

    @pl.when(j == 3)
    def _():
        q = _rope_tile(acc, cos_ref[...], sin_ref[...]) if rope else acc
        pda_ref[0] = (q * DA_QSCALE).astype(BF16)

    @pl.when(j == 4)
    def _():
        k = _rope_tile(acc, cos_ref[...], sin_ref[...]) if rope else acc
        pda_ref[0] = k.astype(BF16)

    @pl.when(j == 5)
    def _():
        pda_ref[0] = acc.astype(BF16)


def _proj(x, sh, sc, w_main, w_g, b_g, rope_tabs, tm):
    bsz, n, d = x.shape
    per_batch = sh.shape[0] == bsz
    rope = rope_tabs is not None
    mod_map = (lambda b, i, j: (b, 0, 0)) if per_batch else (lambda b, i, j: (0, 0, 0))
    in_specs = [pl.BlockSpec((1, tm, d), lambda b, i, j: (b, i, 0)),
                pl.BlockSpec((1, 1, d), mod_map),
                pl.BlockSpec((1, 1, d), mod_map),
                pl.BlockSpec((d, SEC), lambda b, i, j: (0, j)),
                pl.BlockSpec((d, LANES), lambda b, i, j: (0, 0)),
                pl.BlockSpec((1, LANES), lambda b, i, j: (0, 0))]
    args = [x, sh, sc, w_main, w_g, b_g]
    if rope:
        in_specs += [pl.BlockSpec((tm, LANES), lambda b, i, j: (i, 0)),
                     pl.BlockSpec((tm, LANES), lambda b, i, j: (i, 0))]
        args += list(rope_tabs)
    return pl.pallas_call(
        functools.partial(_proj_body, rope=rope),
        grid=(bsz, n // tm, 6),
        in_specs=in_specs,
        out_specs=[pl.BlockSpec((1, tm, SEC), lambda b, i, j: (b, i, jnp.minimum(j, 2))),
                   pl.BlockSpec((1, tm, SEC), lambda b, i, j: (b, i, jnp.maximum(j - 3, 0))),
                   pl.BlockSpec((1, tm, LANES), lambda b, i, j: (b, i, 0))],
        out_shape=[jax.ShapeDtypeStruct((bsz, n, 3 * SEC), F32),
                   jax.ShapeDtypeStruct((bsz, n, 3 * SEC), BF16),
                   jax.ShapeDtypeStruct((bsz, n, LANES), F32)],
        scratch_shapes=[pltpu.VMEM((tm, d), BF16)],
        compiler_params=pltpu.CompilerParams(dimension_semantics=("parallel", "parallel", "arbitrary"),
                                             vmem_limit_bytes=VMEM_LIMIT),
        name="in_proj_rope" if rope else "in_proj",
    )(*args)


def _conv_body(prev_ref, cur_ref, next_ref, w_ref, b_ref, s_ref, o_ref):
    i = pl.program_id(1)
    last = pl.num_programs(1) - 1
    cur = cur_ref[0]
    t = cur.shape[0]
    prev = jnp.where(i > 0, prev_ref[0], 0.0)
    nxt = jnp.where(i < last, next_ref[0], 0.0)
    ext = jnp.concatenate([prev, cur, nxt], axis=0)
    acc = jnp.zeros_like(cur) + b_ref[...]
    half = ML_CONV_W // 2
    for k in range(ML_CONV_W):
        off = SUBLANES - half + k
        acc = acc + ext[off:off + t] * w_ref[k:k + 1, :]
    y = acc * _sigmoid(acc)
    o_ref[0] = (y * s_ref[...]).astype(BF16)


def _conv(p_ml, conv_w, conv_b, tt):
    bsz, n, _ = p_ml.shape
    nb8 = n // SUBLANES
    r = tt // SUBLANES
    scale = jnp.concatenate([jnp.full((1, SEC // 2), ML_DQK ** -0.5, F32), jnp.ones((1, SEC // 2), F32)], axis=1)
    return pl.pallas_call(
        _conv_body,
        grid=(bsz, n // tt),
        in_specs=[pl.BlockSpec((1, SUBLANES, SEC), lambda b, i: (b, jnp.maximum(i * r - 1, 0), 0)),
                  pl.BlockSpec((1, tt, SEC), lambda b, i: (b, i, 0)),
                  pl.BlockSpec((1, SUBLANES, SEC), lambda b, i: (b, jnp.minimum((i + 1) * r, nb8 - 1), 0)),
                  pl.BlockSpec((ML_CONV_W, SEC), lambda b, i: (0, 0)),
                  pl.BlockSpec((1, SEC), lambda b, i: (0, 0)),
                  pl.BlockSpec((1, SEC), lambda b, i: (0, 0))],
        out_specs=pl.BlockSpec((1, tt, SEC), lambda b, i: (b, i, 0)),
        out_shape=jax.ShapeDtypeStruct((bsz, n, SEC), BF16),
        compiler_params=pltpu.CompilerParams(dimension_semantics=("parallel", "parallel"),
                                             vmem_limit_bytes=VMEM_LIMIT),
        name="ml_conv",
    )(p_ml, p_ml, p_ml, conv_w, conv_b.reshape(1, SEC), scale)


def _mlstm_head(q, kt, v_ext, b_col, i_row, b_row, b_tot, mask, c_prev, m_prev):
    dmat = jnp.where(mask, b_col + (i_row - b_row), -jnp.inf)
    m_inter = b_col + m_prev
    m_t = jnp.maximum(m_inter, jnp.max(dmat, axis=-1, keepdims=True))
    w_inter = jnp.exp(m_inter - m_t)
    w = jnp.exp(dmat - m_t) * _dot(q, kt)
    ext = w_inter * _dot(q, c_prev.astype(BF16)) + _dot(w.astype(BF16), v_ext)
    h = ext[:, :ML_DV] / jnp.maximum(jnp.abs(ext[:, ML_DV:ML_DV + 1]), jnp.exp(-m_t))
    gdec_row = b_tot - b_row + i_row
    m_new = jnp.maximum(b_tot + m_prev, jnp.max(gdec_row, axis=-1, keepdims=True))
    a = jnp.exp(b_tot + m_prev - m_new)
    kwt = (kt.astype(F32) * jnp.exp(gdec_row - m_new)).astype(BF16)
    c_new = a * c_prev + _dot(kwt, v_ext)
    return h, c_new, m_new


def _mlstm_body(qf_ref, ktf_ref, vf_ref, gf_ref, gtf_ref, qb_ref, ktb_ref, vb_ref, gb_ref, gtb_ref,
                c0_ref, m0_ref, hf_ref, hb_ref, c_ref, m_ref):
    @pl.when(pl.program_id(1) == 0)
    def _():
        c_ref[...] = c0_ref[...]
        m_ref[...] = m0_ref[...]

    L = ML_CHUNK
    r = lax.broadcasted_iota(I32, (L, L), 0)
    s = lax.broadcasted_iota(I32, (L, L), 1)
    mle = r <= s
    mge = r >= s
    mle_b = mle.astype(BF16)
    mge_b = mge.astype(BF16)
    ones_blk = (lax.broadcasted_iota(I32, (L, ML_DV), 1) == 0).astype(BF16)
    dirs = ((qf_ref, ktf_ref, vf_ref, gf_ref, gtf_ref, hf_ref), (qb_ref, ktb_ref, vb_ref, gb_ref, gtb_ref, hb_ref))
    for d, (q_ref, kt_ref, v_ref, g_ref, gt_ref, h_ref) in enumerate(dirs):
        g = g_ref[0]
        gt = gt_ref[0]
        bc = _dot_exact_lhs(mge_b if d == 0 else mle_b, _log_sigmoid(g))
        br = _dot_exact_rhs(_log_sigmoid(gt), mle_b if d == 0 else mge_b)
        mask = mge if d == 0 else mle
        qa = q_ref[0]
        kta = kt_ref[0]
        va = v_ref[0].astype(BF16)
        outs = []
        for h in range(N_HEADS):
            ci = 2 * N_HEADS * d + h
            cf = ci + N_HEADS
            b_row = br[cf:cf + 1, :]
            b_tot = b_row[:, L - 1:L] if d == 0 else b_row[:, 0:1]
            v_ext = jnp.concatenate([va[:, h * ML_DV:(h + 1) * ML_DV], ones_blk], axis=1)
            outs.append(_mlstm_head(
                qa[:, h * ML_DQK:(h + 1) * ML_DQK], kta[h * ML_DQK:(h + 1) * ML_DQK, :], v_ext,
                bc[:, cf:cf + 1], gt[ci:ci + 1, :], b_row, b_tot, mask,
                c_ref[0, d, h], m_ref[0, d, h][:, 0:1]))
        h_ref[0] = jnp.concatenate([o[0] for o in outs], axis=1)
        c_ref[0, d] = jnp.stack([o[1] for o in outs])
        m_ref[0, d] = jnp.stack([jnp.broadcast_to(o[2], (1, LANES)) for o in outs])


def _mlstm(qkc, kt, p_ml, gates, gates_t, c0, m0):
    bsz, n, _ = qkc.shape
    L = ML_CHUNK
    nc = n // L
    hq = SEC // 2
    fwd = lambda b, i: (b, i, 0)
    bwd = lambda b, i: (b, nc - 1 - i, 0)
    st = lambda b, i: (b, 0, 0, 0, 0)
    c_spec = pl.BlockSpec((1, 2, N_HEADS, ML_DQK, 2 * ML_DV), st)
    m_spec = pl.BlockSpec((1, 2, N_HEADS, 1, LANES), st)

    def dir_specs(pos):
        return [pl.BlockSpec((1, L, hq), lambda b, i: (b, pos(i), 0)),
                pl.BlockSpec((1, hq, L), lambda b, i: (b, 0, pos(i))),
                pl.BlockSpec((1, L, SEC), lambda b, i: (b, pos(i), 1)),
                pl.BlockSpec((1, L, LANES), lambda b, i: (b, pos(i), 0)),
                pl.BlockSpec((1, 4 * N_HEADS, L), lambda b, i: (b, 0, pos(i)))]

    return pl.pallas_call(
        _mlstm_body,
        grid=(bsz, nc),
        in_specs=dir_specs(lambda i: i) + dir_specs(lambda i: nc - 1 - i) + [c_spec, m_spec],
        out_specs=[pl.BlockSpec((1, L, SEC), fwd), pl.BlockSpec((1, L, SEC), bwd), c_spec, m_spec],
        out_shape=[jax.ShapeDtypeStruct((bsz, n, SEC), F32), jax.ShapeDtypeStruct((bsz, n, SEC), F32),
                   jax.ShapeDtypeStruct(c0.shape, F32), jax.ShapeDtypeStruct(m0.shape, F32)],
        compiler_params=pltpu.CompilerParams(dimension_semantics=("parallel", "arbitrary"),
                                             vmem_limit_bytes=VMEM_LIMIT),
        name="mlstm",
    )(qkc, kt, p_ml, gates, gates_t, qkc, kt, p_ml, gates, gates_t, c0, m0)


def _attn_body(lam_ref, q_ref, k_ref, vt_ref, o_ref, qq_s, sa, sb, m_s, acc_s, *, tq, tk, lambda_init):
    nk = k_ref.shape[1] // tk
    nq = q_ref.shape[1] // tq
    blocks_per_iter = 1 if nk % 2 == 0 else 2
    assert nq % blocks_per_iter == 0
    bufs = (sa, sb)
    lm = lam_ref[...]
    lam = (jnp.exp(jnp.sum(lm[0:1] * lm[1:2], axis=-1, keepdims=True))
           - jnp.exp(jnp.sum(lm[2:3] * lm[3:4], axis=-1, keepdims=True)) + lambda_init)

    def scores(blk, t, s_ref):
        if t == 0:
            qoff = pl.multiple_of(jnp.minimum(blk, nq - 1) * tq, LANES)
            qt = jnp.transpose(q_ref[0, pl.ds(qoff, tq), :])
            row = lax.broadcasted_iota(I32, qt.shape, 0)
            zero = jnp.zeros_like(qt)
            qq_s[...] = jnp.concatenate([jnp.where(row < DA_DQK, qt, zero), jnp.where(row >= DA_DQK, qt, zero)],
                                        axis=1)
        s_ref[...] = _dot(k_ref[0, t * tk:(t + 1) * tk, :], qq_s[...])

    def absorb(blk, t, s_ref):
        vts = vt_ref[0, 0, :, t * tk:(t + 1) * tk]
        mt = jnp.max(s_ref[...].reshape(tk // SUBLANES, SUBLANES, 2 * tq).max(axis=0), axis=0, keepdims=True)
        if t == 0:
            m_new = mt
            acc = _dot(vts, jnp.exp2(s_ref[...] - m_new).astype(BF16))
        else:
            m_old = m_s[...]
            m_new = jnp.maximum(m_old, mt)
            p = jnp.exp2(s_ref[...] - m_new)
            acc = jnp.exp2(m_old - m_new) * acc_s[...] + _dot(vts, p.astype(BF16))
        if t == nk - 1:
            o2 = acc[:DA_DV] / acc[DA_DV:DA_DV + 1]
            qoff = pl.multiple_of(blk * tq, LANES)
            o_ref[0, pl.ds(qoff, tq), :] = jnp.transpose(o2[:, :tq] - lam * o2[:, tq:])
        else:
            acc_s[...] = acc
            m_s[...] = m_new

    scores(0, 0, bufs[0])

    def step(j, carry):
        for b in range(blocks_per_iter):
            blk = j * blocks_per_iter + b
            for t in range(nk):
                cur = (b * nk + t) % 2
                if t + 1 < nk:
                    scores(blk, t + 1, bufs[1 - cur])
                else:
                    scores(blk + 1, 0, bufs[1 - cur])
                absorb(blk, t, bufs[cur])
        return carry

    lax.fori_loop(0, nq // blocks_per_iter, step, 0)


def _attn(da_lambda, p_da, k_all, vt, tq, tk, lambda_init):
    bsz, n, _ = p_da.shape
    hd = 2 * DA_DQK
    m = k_all.shape[1]
    return pl.pallas_call(
        functools.partial(_attn_body, tq=tq, tk=tk, lambda_init=lambda_init),
        grid=(bsz, N_HEADS),
        in_specs=[pl.BlockSpec((4, DA_DQK), lambda b, h: (0, 0)),
                  pl.BlockSpec((1, n, hd), lambda b, h: (b, 0, h)),
                  pl.BlockSpec((1, m, hd), lambda b, h: (b, 0, h)),
                  pl.BlockSpec((1, 1, DA_VROWS, m), lambda b, h: (b, h, 0, 0))],
        out_specs=pl.BlockSpec((1, n, DA_DV), lambda b, h: (b, 0, h)),
        out_shape=jax.ShapeDtypeStruct((bsz, n, SEC), F32),
        scratch_shapes=[pltpu.VMEM((hd, 2 * tq), BF16),
                        pltpu.VMEM((tk, 2 * tq), F32), pltpu.VMEM((tk, 2 * tq), F32),
                        pltpu.VMEM((1, 2 * tq), F32), pltpu.VMEM((DA_VROWS, 2 * tq), F32)],
        compiler_params=pltpu.CompilerParams(dimension_semantics=("parallel", "parallel"),
                                             vmem_limit_bytes=VMEM_LIMIT),
        name="diff_attn",
    )(da_lambda, p_da, k_all, vt)


def _merge_body(hf_ref, hb_ref, o_ref, od_ref, x_ref, wout_ref, mlw_ref, daw_ref, g1_ref, ln1w_ref, ln1b_ref,
                sh2_ref, sc2_ref, wr_ref, h1_ref, u_ref, aff_ref, *, lambda_init, n_experts):
    hsum = hf_ref[0] + hb_ref[0]
    og = o_ref[0]
    od = od_ref[0]
    mlw = mlw_ref[...]
    daw = daw_ref[...]
    parts = []
    for h in range(N_HEADS):
        sl = slice(h * ML_DV, (h + 1) * ML_DV)
        parts.append(_ln_rows(hsum[:, sl]) * mlw[:, sl] * _sigmoid(og[:, sl]))
    for h in range(N_HEADS):
        sl = slice(h * DA_DV, (h + 1) * DA_DV)
        z = od[:, sl]
        zn = z * lax.rsqrt(jnp.mean(z * z, axis=-1, keepdims=True) + LN_EPS)
        parts.append(zn * daw[:, sl] * (1.0 - lambda_init))
    ycat = jnp.concatenate(parts, axis=1).astype(BF16)
    mix = _dot(ycat, wout_ref[...])
    h1 = _ln_rows(DEEPNORM_ALPHA * x_ref[0] + g1_ref[0] * mix) * ln1w_ref[...] + ln1b_ref[...]
    h1_ref[0] = h1
    u = _ln_rows(h1) * (1.0 + sc2_ref[0]) + sh2_ref[0]
    u_ref[0] = u
    uhi, umid, _ = _split3(u)
    whi, wmid, _ = _split3(wr_ref[...])
    logits = _dot(uhi, whi) + _dot(uhi, wmid) + _dot(umid, whi)
    lane = lax.broadcasted_iota(I32, logits.shape, 1)
    logits = jnp.where(lane < n_experts, logits, -jnp.inf)
    e = jnp.exp(logits - jnp.max(logits, axis=-1, keepdims=True))
    aff_ref[0] = e / jnp.sum(e, axis=-1, keepdims=True)


def _merge(hf, hb, p_ml, od, x, w_out, ml_norm_w, da_norm_w, g1, ln1_w, ln1_b, sh2, sc2, w_router_pad,
           tm, lambda_init, n_experts):
    bsz, n, d = x.shape
    row = lambda b, i: (b, i, 0)
    vec = lambda b, i: (0, 0)
    mod = lambda b, i: (b, 0, 0)
    return pl.pallas_call(
        functools.partial(_merge_body, lambda_init=lambda_init, n_experts=n_experts),
        grid=(bsz, n // tm),
        in_specs=[pl.BlockSpec((1, tm, SEC), row), pl.BlockSpec((1, tm, SEC), row),
                  pl.BlockSpec((1, tm, SEC), lambda b, i: (b, i, 2)),
                  pl.BlockSpec((1, tm, SEC), row),
                  pl.BlockSpec((1, tm, d), row),
                  pl.BlockSpec((2 * SEC, d), vec),
                  pl.BlockSpec((1, SEC), vec), pl.BlockSpec((1, SEC), vec),
                  pl.BlockSpec((1, 1, d), mod),
                  pl.BlockSpec((1, d), vec), pl.BlockSpec((1, d), vec),
                  pl.BlockSpec((1, 1, d), mod), pl.BlockSpec((1, 1, d), mod),
                  pl.BlockSpec((d, LANES), vec)],
        out_specs=[pl.BlockSpec((1, tm, d), row), pl.BlockSpec((1, tm, d), row),
                   pl.BlockSpec((1, tm, LANES), row)],
        out_shape=[jax.ShapeDtypeStruct((bsz, n, d), F32), jax.ShapeDtypeStruct((bsz, n, d), F32),
                   jax.ShapeDtypeStruct((bsz, n, LANES), F32)],
        compiler_params=pltpu.CompilerParams(dimension_semantics=("parallel", "parallel"),
                                             vmem_limit_bytes=VMEM_LIMIT),
        name="merge_ln1_router",
    )(hf, hb, p_ml, od, x, w_out, ml_norm_w.reshape(1, SEC), da_norm_w.reshape(1, SEC), g1,
      ln1_w.reshape(1, d), ln1_b.reshape(1, d), sh2, sc2, w_router_pad)


def _cumsum_tokens(maskf, uincl_b, lstrict_b):
    mb = maskf.astype(BF16)
    win = _dot(mb, uincl_b)
    tot = jnp.broadcast_to(win[:, LANES - 1:LANES], win.shape)
    offs = _dot(lstrict_b, tot.astype(BF16))
    return offs + win, tot


def _router_body(a_ref, idx_ref, g_ref, *, cap):
    a = a_ref[0]
    rr = a.shape[0]
    bits = lax.bitcast_convert_type(a, I32)
    cur = jnp.zeros((1, 1), I32)
    for bit in range(30, -1, -1):
        cand = cur | (1 << bit)
        cnt = jnp.sum((bits >= cand).astype(I32), keepdims=True)
        cur = jnp.where(cnt >= cap, cand, cur)
    gt = bits > cur
    eq = bits == cur
    need = (cap - jnp.sum(gt.astype(I32), keepdims=True)).astype(F32)

    li = lax.broadcasted_iota(I32, (LANES, LANES), 0)
    lj = lax.broadcasted_iota(I32, (LANES, LANES), 1)
    uincl_b = (li <= lj).astype(BF16)
    ri = lax.broadcasted_iota(I32, (rr, rr), 0)
    rj = lax.broadcasted_iota(I32, (rr, rr), 1)
    lstrict_b = (rj < ri).astype(BF16)
    rincl_b = (ri <= rj).astype(BF16)

    eqf = eq.astype(F32)
    eq_incl, _ = _cumsum_tokens(eqf, uincl_b, lstrict_b)
    sel = gt | (eq & ((eq_incl - eqf) < need))
    self_ = sel.astype(F32)
    csum, _ = _cumsum_tokens(self_, uincl_b, lstrict_b)

    ones8 = jnp.ones((SUBLANES, LANES), BF16)
    tot_lane = lax.dot_general(ones8, self_.astype(BF16), (((1,), (1,)), ((), ())),
                               preferred_element_type=F32)
    end_lane = _dot(tot_lane.astype(BF16), rincl_b)[0:1]
    beg_lane = end_lane - tot_lane[0:1]
    j = lax.broadcasted_iota(I32, (cap, 1), 0).astype(F32)
    onehot = ((beg_lane <= j) & (end_lane > j)).astype(BF16)
    row_idx = jnp.sum((end_lane <= j).astype(F32), axis=-1, keepdims=True)
    chi = jnp.floor(csum * (1.0 / LANES))
    clo = csum - chi * LANES
    crow = _dot(onehot, chi.astype(BF16)) * LANES + _dot(onehot, clo.astype(BF16))
    lane_idx = jnp.sum((crow <= j).astype(F32), axis=-1, keepdims=True)
    idx_ref[0] = (row_idx * LANES + lane_idx).astype(I32)
    ahi, amid, alo = _split3(a)
    arow = _dot(onehot, ahi) + _dot(onehot, amid) + _dot(onehot, alo)
    lane = lax.broadcasted_iota(I32, (cap, LANES), 1).astype(F32)
    g_ref[0] = jnp.sum(jnp.where(lane == lane_idx, arow, 0.0), axis=-1, keepdims=True)


def _router(aff_t, cap):
    g, rr, _ = aff_t.shape
    return pl.pallas_call(
        functools.partial(_router_body, cap=cap),
        grid=(g,),
        in_specs=[pl.BlockSpec((1, rr, LANES), lambda i: (i, 0, 0))],
        out_specs=[pl.BlockSpec((1, cap, 1), lambda i: (i, 0, 0)), pl.BlockSpec((1, cap, 1), lambda i: (i, 0, 0))],
        out_shape=[jax.ShapeDtypeStruct((g, cap, 1), I32), jax.ShapeDtypeStruct((g, cap, 1), F32)],
        compiler_params=pltpu.CompilerParams(dimension_semantics=("parallel",), vmem_limit_bytes=VMEM_LIMIT),
        name="ec_select",
    )(aff_t)


def _moe_body(idx_ref, g_ref, u_hbm, wg_ref, wu_ref, wd_ref, y_in_hbm, y_hbm, buf, xs, tmp, sem,
              *, cap, n_tok, n_experts, rc):
    del y_in_hbm
    grp = pl.program_id(0)
    h = pl.program_id(1)
    base = (grp // n_experts) * n_tok
    d = xs.shape[1]

    @pl.when(h == 0)
    def _():
        def issue(i, carry):
            for r in range(SUBLANES):
                row = base + idx_ref[0, 0, i * SUBLANES + r]
                pltpu.make_async_copy(u_hbm.at[pl.ds(row, 1)], buf.at[i, pl.ds(r, 1)], sem.at[0]).start()
            return carry
        lax.fori_loop(0, cap // SUBLANES, issue, 0)

        def drain(i, carry):
            pltpu.make_async_copy(u_hbm.at[pl.ds(0, SUBLANES)], buf.at[i], sem.at[0]).wait()
            return carry
        lax.fori_loop(0, cap // SUBLANES, drain, 0)
        xs[...] = buf[...].reshape(cap, d).astype(BF16)
        buf[...] = jnp.zeros(buf.shape, F32)

    x = xs[...]
    a = _dot(x, wg_ref[0].astype(BF16))
    b = _dot(x, wu_ref[0].astype(BF16))
    hid = (a * _sigmoid(a) * b).astype(BF16)
    buf[...] += _dot(hid, wd_ref[0].astype(BF16)).reshape(buf.shape)

    def y_rows(c, slot, to_hbm):
        def issue(i, carry):
            for r in range(SUBLANES):
                row = base + idx_ref[0, 0, c * rc + i * SUBLANES + r]
                hbm_row = y_hbm.at[pl.ds(row, 1)]
                vmem_row = tmp.at[slot, i, pl.ds(r, 1)]
                if to_hbm:
                    pltpu.make_async_copy(vmem_row, hbm_row, sem.at[3 + slot]).start()
                else:
                    pltpu.make_async_copy(hbm_row, vmem_row, sem.at[1 + slot]).start()
            return carry
        lax.fori_loop(0, rc // SUBLANES, issue, 0)

    def y_wait(slot, to_hbm):
        def drain(i, carry):
            hbm_rows = y_hbm.at[pl.ds(0, SUBLANES)]
            if to_hbm:
                pltpu.make_async_copy(tmp.at[slot, i], hbm_rows, sem.at[3 + slot]).wait()
            else:
                pltpu.make_async_copy(hbm_rows, tmp.at[slot, i], sem.at[1 + slot]).wait()
            return carry
        lax.fori_loop(0, rc // SUBLANES, drain, 0)

    @pl.when(h == pl.num_programs(1) - 1)
    def _():
        n_chunks = cap // rc
        y_rows(0, 0, False)
        for c in range(n_chunks):
            slot = c % 2
            if c + 1 < n_chunks:
                if c >= 1:
                    y_wait(1 - slot, True)
                y_rows(c + 1, 1 - slot, False)
            y_wait(slot, False)
            acc = buf[c * rc // SUBLANES:(c + 1) * rc // SUBLANES].reshape(rc, d)
            tmp[slot] = tmp[slot] + (g_ref[0, c * rc:(c + 1) * rc, :] * acc).reshape(tmp.shape[1:])
            y_rows(c, slot, True)
        for c in range(max(n_chunks - 2, 0), n_chunks):
            y_wait(c % 2, True)


def _moe(idx3, gsel, u2, w_gate, w_up, w_down, y0, n_tok, th):
    g, _, cap = idx3.shape
    n_experts, d, de = w_gate.shape
    rc = min(256, max(cap // 4, SUBLANES))
    return pl.pallas_call(
        functools.partial(_moe_body, cap=cap, n_tok=n_tok, n_experts=n_experts, rc=rc),
        grid=(g, de // th),
        in_specs=[pl.BlockSpec((1, 1, cap), lambda i, h: (i, 0, 0), memory_space=pltpu.SMEM),
                  pl.BlockSpec((1, cap, 1), lambda i, h: (i, 0, 0)),
                  pl.BlockSpec(memory_space=pl.ANY),
                  pl.BlockSpec((1, d, th), lambda i, h: (i % n_experts, 0, h)),
                  pl.BlockSpec((1, d, th), lambda i, h: (i % n_experts, 0, h)),
                  pl.BlockSpec((1, th, d), lambda i, h: (i % n_experts, h, 0)),
                  pl.BlockSpec(memory_space=pl.ANY)],
        out_specs=pl.BlockSpec(memory_space=pl.ANY),
        out_shape=jax.ShapeDtypeStruct(y0.shape, F32),
        scratch_shapes=[pltpu.VMEM((cap // SUBLANES, SUBLANES, d), F32), pltpu.VMEM((cap, d), BF16),
                        pltpu.VMEM((2, rc // SUBLANES, SUBLANES, d), F32),
                        pltpu.SemaphoreType.DMA((5,))],
        input_output_aliases={6: 0},
        compiler_params=pltpu.CompilerParams(dimension_semantics=("arbitrary", "arbitrary"),
                                             vmem_limit_bytes=VMEM_LIMIT, disable_bounds_checks=True),
        name="ec_moe",
    )(idx3, gsel, u2, w_gate, w_up, w_down, y0)


def _final_body(h1_ref, y_ref, g2_ref, w_ref, b_ref, o_ref):
    t = DEEPNORM_ALPHA * h1_ref[0] + g2_ref[0] * y_ref[0]
    o_ref[0] = _ln_rows(t) * w_ref[...] + b_ref[...]


def _final(h1, y, g2, ln2_w, ln2_b, tm):
    bsz, n, d = h1.shape
    row = lambda b, i: (b, i, 0)
    return pl.pallas_call(
        _final_body,
        grid=(bsz, n // tm),
        in_specs=[pl.BlockSpec((1, tm, d), row), pl.BlockSpec((1, tm, d), row),
                  pl.BlockSpec((1, 1, d), lambda b, i: (b, 0, 0)),
                  pl.BlockSpec((1, d), lambda b, i: (0, 0)), pl.BlockSpec((1, d), lambda b, i: (0, 0))],
        out_specs=pl.BlockSpec((1, tm, d), row),
        out_shape=jax.ShapeDtypeStruct((bsz, n, d), F32),
        compiler_params=pltpu.CompilerParams(dimension_semantics=("parallel", "parallel"),
                                             vmem_limit_bytes=VMEM_LIMIT),
        name="ln2",
    )(h1, y, g2, ln2_w.reshape(1, d), ln2_b.reshape(1, d))


def _rope_tables(n):
    rows = n // GRID_W
    pos_row = jnp.repeat(jnp.arange(rows, dtype=I32), GRID_W).astype(F32)
    pos_col = jnp.tile(jnp.arange(GRID_W, dtype=I32), rows).astype(F32)
    n_freq = DA_DQK // 4
    inv = ROPE_BASE ** (-jnp.arange(n_freq, dtype=F32) / n_freq)
    ang = jnp.concatenate([pos_row[:, None] * inv, pos_col[:, None] * inv], -1)
    cos, sin = jnp.cos(ang), jnp.sin(ang)
    reps = LANES // DA_DQK
    cosf = jnp.tile(cos, (1, 2 * reps))
    sins = jnp.tile(jnp.concatenate([-sin, sin], -1), (1, reps))
    return cosf, sins


def _pick(n, prefs):
    for t in prefs:
        if n % t == 0:
            return t
    return n


def kernel(x, c, ctx, c_ctx, w_ada, b_ada, w_in, b_gates, ml_conv_w, ml_conv_b, ml_norm_w, da_lambda, da_norm_w,
           w_out, ln1_w, ln1_b, w_router, w_gate, w_up, w_down, ln2_w, ln2_b):
    bsz, n, d = x.shape
    n_ctx = ctx.shape[1]
    n_experts = w_gate.shape[1]
    cap = EC_CAPACITY_FACTOR * n // n_experts
    l = 0
    lambda_init = 0.8 - 0.6 * math.exp(-0.3 * l)

    cin = jnp.zeros((SUBLANES, d), F32).at[:bsz].set(c).at[bsz].set(c_ctx)
    mod = _ada(cin, w_ada[l], b_ada[l])
    sh1, sc1, g1, sh2, sc2, g2 = (mod[:bsz, k * d:(k + 1) * d][:, None, :] for k in range(6))
    csh1, csc1 = (mod[bsz:bsz + 1, k * d:(k + 1) * d][:, None, :] for k in range(2))

    w = w_in[l]
    off_gates = 3 * SEC
    n_gate = 4 * N_HEADS
    w_main = jnp.concatenate([w[:, :off_gates], w[:, off_gates + n_gate:]], axis=1).astype(BF16)
    w_g = jnp.pad(w[:, off_gates:off_gates + n_gate], ((0, 0), (0, LANES - n_gate))).astype(BF16)
    b_g = jnp.pad(b_gates[l], (0, LANES - n_gate)).reshape(1, LANES)
    pml_l, pda_l, gates_l = _proj(x, sh1, sc1, w_main, w_g, b_g, _rope_tables(n), _pick(n, (512, 256, 128)))
    pml_c, pda_c, gates_c = _proj(ctx, csh1, csc1, w_main, w_g, b_g, None, _pick(n_ctx, (512, 256, 128)))

    qkc_l = _conv(pml_l, ml_conv_w[l], ml_conv_b[l], _pick(n, (512, 256, 128)))
    qkc_c = _conv(pml_c, ml_conv_w[l], ml_conv_b[l], _pick(n_ctx, (512, 256, 128)))
    gt_l = jnp.swapaxes(gates_l[:, :, :n_gate], 1, 2)
    gt_c = jnp.swapaxes(gates_c[:, :, :n_gate], 1, 2)
    kt_l = jnp.swapaxes(qkc_l[:, :, SEC // 2:], 1, 2)
    kt_c = jnp.swapaxes(qkc_c[:, :, SEC // 2:], 1, 2)
    c0 = jnp.zeros((bsz, 2, N_HEADS, ML_DQK, 2 * ML_DV), F32)
    m0 = jnp.zeros((bsz, 2, N_HEADS, 1, LANES), F32)
    _, _, c1, m1 = _mlstm(qkc_c, kt_c, pml_c, gates_c, gt_c, c0, m0)
    hf, hb, _, _ = _mlstm(qkc_l, kt_l, pml_l, gates_l, gt_l, c1, m1)

    m_all = n + n_ctx
    k_all = jnp.concatenate([pda_l[:, :, SEC:2 * SEC], pda_c[:, :, SEC:2 * SEC]], axis=1)
    v_all = jnp.concatenate([pda_l[:, :, 2 * SEC:], pda_c[:, :, 2 * SEC:]], axis=1)
    vt = v_all.reshape(bsz, m_all, N_HEADS, DA_DV).transpose(0, 2, 3, 1)
    ones_rows = jnp.zeros((bsz, N_HEADS, DA_VROWS - DA_DV, m_all), BF16).at[:, :, 0].set(1.0)
    vt = jnp.concatenate([vt, ones_rows], axis=2)
    od = _attn(da_lambda[l], pda_l, k_all, vt, _pick(n, (256, 128)), _pick(m_all, (1280, 256, 128)), lambda_init)

    w_router_pad = jnp.pad(w_router[l], ((0, 0), (0, LANES - n_experts)))
    h1, u, aff = _merge(hf, hb, pml_l, od, x, w_out[l].astype(BF16), ml_norm_w[l], da_norm_w[l], g1,
                        ln1_w[l], ln1_b[l], sh2, sc2, w_router_pad, _pick(n, (256, 128)), lambda_init, n_experts)

    aff_t = jnp.swapaxes(aff[:, :, :n_experts], 1, 2).reshape(bsz * n_experts, n // LANES, LANES)
    idx, gsel = _router(aff_t, cap)
    y = _moe(idx.reshape(bsz * n_experts, 1, cap), gsel, u.reshape(bsz * n, d), w_gate[l], w_up[l], w_down[l],
             jnp.zeros((bsz * n, d), F32), n, _pick(w_gate.shape[3], (256, 128)))
    return _final(h1, y.reshape(bsz, n, d), g2, ln2_w[l], ln2_b[l], _pick(n, (512, 256, 128)))
```

```python
import functools
import math

import jax
import jax.numpy as jnp
from jax import lax
from jax.experimental import pallas as pl
from jax.experimental.pallas import tpu as pltpu

F32 = jnp.float32
BF16 = jnp.bfloat16
I32 = jnp.int32

LANES = 128
SUBLANES = 8
GRID_W = 64
N_HEADS = 8
ML_DQK = 64
ML_DV = 128
ML_CONV_W = 5
ML_CHUNK = 128
DA_DQK = 64
DA_DV = 128
SEC = 1024
ROPE_BASE = 10000.0
LN_EPS = 1e-5
DEPTH = 1
DEEPNORM_ALPHA = (2.0 * DEPTH) ** 0.25
EC_CAPACITY_FACTOR = 2
VMEM_LIMIT = 56 * 1024 * 1024
DA_QSCALE = DA_DQK ** -0.5 * math.log2(math.e)
DA_VROWS = DA_DV + 16


def _sigmoid(x):
    return 1.0 / (1.0 + jnp.exp(-x))


def _log_sigmoid(x):
    return jnp.minimum(x, 0.0) - jnp.log(1.0 + jnp.exp(-jnp.abs(x)))


def _ln_rows(x):
    mu = jnp.mean(x, axis=-1, keepdims=True)
    xc = x - mu
    var = jnp.mean(xc * xc, axis=-1, keepdims=True)
    return xc * lax.rsqrt(var + LN_EPS)


def _split3(x):
    hi = x.astype(BF16)
    r1 = x - hi.astype(F32)
    mid = r1.astype(BF16)
    lo = (r1 - mid.astype(F32)).astype(BF16)
    return hi, mid, lo


def _dot(a, b):
    return jnp.dot(a, b, preferred_element_type=F32)


def _dot_exact_lhs(a_bf16, x_f32):
    hi, mid, lo = _split3(x_f32)
    return _dot(a_bf16, hi) + _dot(a_bf16, mid) + _dot(a_bf16, lo)


def _dot_exact_rhs(x_f32, a_bf16):
    hi, mid, lo = _split3(x_f32)
    return _dot(hi, a_bf16) + _dot(mid, a_bf16) + _dot(lo, a_bf16)


def _ada_body(c_ref, w_ref, b_ref, o_ref):
    c = c_ref[...]
    a = c * _sigmoid(c)
    hi, mid, lo = _split3(a)
    w = w_ref[...]
    whi, wmid, wlo = _split3(w)
    acc = _dot(hi, whi) + _dot(hi, wmid) + _dot(mid, whi)
    acc = acc + _dot(mid, wmid) + _dot(hi, wlo) + _dot(lo, whi)
    o_ref[...] = acc + b_ref[...]


def _ada(cin, w, b):
    rows, d = cin.shape
    n = w.shape[1]
    tn = _pick(n, (1024, 512, 256, 128))
    return pl.pallas_call(
        _ada_body,
        grid=(n // tn,),
        in_specs=[pl.BlockSpec((rows, d), lambda j: (0, 0)),
                  pl.BlockSpec((d, tn), lambda j: (0, j)),
                  pl.BlockSpec((1, tn), lambda j: (0, j))],
        out_specs=pl.BlockSpec((rows, tn), lambda j: (0, j)),
        out_shape=jax.ShapeDtypeStruct((rows, n), F32),
        compiler_params=pltpu.CompilerParams(dimension_semantics=("arbitrary",),
                                             vmem_limit_bytes=VMEM_LIMIT),
        name="ada_mod",
    )(cin, w, b.reshape(1, n))


def _rope_tile(acc, cosf, sins):
    outs = []
    lane = lax.broadcasted_iota(I32, (acc.shape[0], LANES), 1)
    first = (lane & (DA_DQK - 1)) < (DA_DQK // 2)
    for t in range(acc.shape[1] // LANES):
        a = acc[:, t * LANES:(t + 1) * LANES]
        partner = jnp.where(first, pltpu.roll(a, LANES - DA_DQK // 2, 1), pltpu.roll(a, DA_DQK // 2, 1))
        outs.append(a * cosf + partner * sins)
    return jnp.concatenate(outs, axis=1)


def _proj_body(*refs, rope):
    if rope:
        (x_ref, sh_ref, sc_ref, w_ref, wg_ref, bg_ref, cos_ref, sin_ref,
         pml_ref, pda_ref, g_ref, xn_ref) = refs
    else:
        (x_ref, sh_ref, sc_ref, w_ref, wg_ref, bg_ref, pml_ref, pda_ref, g_ref, xn_ref) = refs
    j = pl.program_id(2)

    @pl.when(j == 0)
    def _():
        y = _ln_rows(x_ref[0]) * (1.0 + sc_ref[0]) + sh_ref[0]
        yb = y.astype(BF16)
        xn_ref[...] = yb
        g_ref[0] = _dot(yb, wg_ref[...]) + bg_ref[...]

    acc = _dot(xn_ref[...], w_ref[...])

    @pl.when(j < 3)
    def _():
        pml_ref[0] = acc

    @pl.when(j == 3)
    def _():
        q = _rope_tile(acc, cos_ref[...], sin_ref[...]) if rope else acc
        pda_ref[0] = (q * DA_QSCALE).astype(BF16)

    @pl.when(j == 4)
    def _():
        k = _rope_tile(acc, cos_ref[...], sin_ref[...]) if rope else acc
        pda_ref[0] = k.astype(BF16)

    @pl.when(j == 5)
    def _():
        pda_ref[0] = acc.astype(BF16)


def _proj(x, sh, sc, w_main, w_g, b_g, rope_tabs, tm):
    bsz, n, d = x.shape
    per_batch = sh.shape[0] == bsz
    rope = rope_tabs is not None
    mod_map = (lambda b, i, j: (b, 0, 0)) if per_batch else (lambda b, i, j: (0, 0, 0))
    in_specs = [pl.BlockSpec((1, tm, d), lambda b, i, j: (b, i, 0)),
                pl.BlockSpec((1, 1, d), mod_map),
                pl.BlockSpec((1, 1, d), mod_map),
                pl.BlockSpec((d, SEC), lambda b, i, j: (0, j)),
                pl.BlockSpec((d, LANES), lambda b, i, j: (0, 0)),
                pl.BlockSpec((1, LANES), lambda b, i, j: (0, 0))]
    args = [x, sh, sc, w_main, w_g, b_g]
    if rope:
        in_specs += [pl.BlockSpec((tm, LANES), lambda b, i, j: (i, 0)),
                     pl.BlockSpec((tm, LANES), lambda b, i, j: (i, 0))]
        args += list(rope_tabs)
    return pl.pallas_call(
        functools.partial(_proj_body, rope=rope),
        grid=(bsz, n // tm, 6),
        in_specs=in_specs,
        out_specs=[pl.BlockSpec((1, tm, SEC), lambda b, i, j: (b, i, jnp.minimum(j, 2))),
                   pl.BlockSpec((1, tm, SEC), lambda b, i, j: (b, i, jnp.maximum(j - 3, 0))),
                   pl.BlockSpec((1, tm, LANES), lambda b, i, j: (b, i, 0))],
        out_shape=[jax.ShapeDtypeStruct((bsz, n, 3 * SEC), F32),
                   jax.ShapeDtypeStruct((bsz, n, 3 * SEC), BF16),
                   jax.ShapeDtypeStruct((bsz, n, LANES), F32)],
        scratch_shapes=[pltpu.VMEM((tm, d), BF16)],
        compiler_params=pltpu.CompilerParams(dimension_semantics=("parallel", "parallel", "arbitrary"),
                                             vmem_limit_bytes=VMEM_LIMIT),
        name="in_proj_rope" if rope else "in_proj",
    )(*args)


def _conv_body(prev_ref, cur_ref, next_ref, w_ref, b_ref, s_ref, o_ref):
    i = pl.program_id(1)
    last = pl.num_programs(1) - 1
    cur = cur_ref[0]
    t = cur.shape[0]
    prev = jnp.where(i > 0, prev_ref[0], 0.0)
    nxt = jnp.where(i < last, next_ref[0], 0.0)
    ext = jnp.concatenate([prev, cur, nxt], axis=0)
    acc = jnp.zeros_like(cur) + b_ref[...]
    half = ML_CONV_W // 2
    for k in range(ML_CONV_W):
        off = SUBLANES - half + k
        acc = acc + ext[off:off + t] * w_ref[k:k + 1, :]
    y = acc * _sigmoid(acc)
    o_ref[0] = (y * s_ref[...]).astype(BF16)


def _conv(p_ml, conv_w, conv_b, tt):
    bsz, n, _ = p_ml.shape
    nb8 = n // SUBLANES
    r = tt // SUBLANES
    scale = jnp.concatenate([jnp.full((1, SEC // 2), ML_DQK ** -0.5, F32), jnp.ones((1, SEC // 2), F32)], axis=1)
    return pl.pallas_call(
        _conv_body,
        grid=(bsz, n // tt),
        in_specs=[pl.BlockSpec((1, SUBLANES, SEC), lambda b, i: (b, jnp.maximum(i * r - 1, 0), 0)),
                  pl.BlockSpec((1, tt, SEC), lambda b, i: (b, i, 0)),
                  pl.BlockSpec((1, SUBLANES, SEC), lambda b, i: (b, jnp.minimum((i + 1) * r, nb8 - 1), 0)),
                  pl.BlockSpec((ML_CONV_W, SEC), lambda b, i: (0, 0)),
                  pl.BlockSpec((1, SEC), lambda b, i: (0, 0)),
                  pl.BlockSpec((1, SEC), lambda b, i: (0, 0))],
        out_specs=pl.BlockSpec((1, tt, SEC), lambda b, i: (b, i, 0)),
        out_shape=jax.ShapeDtypeStruct((bsz, n, SEC), BF16),
        compiler_params=pltpu.CompilerParams(dimension_semantics=("parallel", "parallel"),
                                             vmem_limit_bytes=VMEM_LIMIT),
        name="ml_conv",
    )(p_ml, p_ml, p_ml, conv_w, conv_b.reshape(1, SEC), scale)


def _mlstm_head(kp, qtm, vt_ext, r_col, b_row, i_row, b_tot, mask_t, c_prev, m_prev):
    dmat = jnp.where(mask_t, b_row + r_col, -jnp.inf)
    m_inter = b_row + m_prev
    m_t = jnp.maximum(m_inter, jnp.max(dmat, axis=0, keepdims=True))
    w_inter = jnp.exp(m_inter - m_t)
    w = jnp.exp(dmat - m_t) * _dot(kp, qtm)
    ext = w_inter * _dot(c_prev.astype(BF16), qtm) + _dot(vt_ext, w.astype(BF16))
    ht = ext[:ML_DV] / jnp.maximum(jnp.abs(ext[ML_DV:ML_DV + 1]), jnp.exp(-m_t))
    gdec = b_tot - b_row + i_row
    m_new = jnp.maximum(b_tot + m_prev, jnp.max(gdec, axis=-1, keepdims=True))
    a = jnp.exp(b_tot + m_prev - m_new)
    vw = (vt_ext.astype(F32) * jnp.exp(gdec - m_new)).astype(BF16)
    c_new = a * c_prev + _dot(vw, kp)
    return jnp.transpose(ht), c_new, m_new


def _mlstm_body(qtf_ref, kf_ref, vtf_ref, gf_ref, gtf_ref, qtb_ref, kb_ref, vtb_ref, gb_ref, gtb_ref,
                c0_ref, m0_ref, hf_ref, hb_ref, c_ref, m_ref):
    @pl.when(pl.program_id(1) == 0)
    def _():
        c_ref[...] = c0_ref[...]
        m_ref[...] = m0_ref[...]

    L = ML_CHUNK
    r = lax.broadcasted_iota(I32, (L, L), 0)
    s = lax.broadcasted_iota(I32, (L, L), 1)
    mle = r <= s
    mge = r >= s
    mle_b = mle.astype(BF16)
    mge_b = mge.astype(BF16)
    ones_rows = (lax.broadcasted_iota(I32, (ML_DV, L), 0) == 0).astype(BF16)
    pair_row = lax.broadcasted_iota(I32, (2 * ML_DQK, L), 0) // ML_DQK
    dirs = ((qtf_ref, kf_ref, vtf_ref, gf_ref, gtf_ref, hf_ref), (qtb_ref, kb_ref, vtb_ref, gb_ref, gtb_ref, hb_ref))
    for d, (qt_ref, k_ref, vt_ref, g_ref, gt_ref, h_ref) in enumerate(dirs):
        g = g_ref[0]
        gt = gt_ref[0]
        bc = _dot_exact_lhs(mge_b if d == 0 else mle_b, _log_sigmoid(g))
        br = _dot_exact_rhs(_log_sigmoid(gt), mle_b if d == 0 else mge_b)
        mask_t = mle if d == 0 else mge
        qta = qt_ref[0]
        ka = k_ref[0]
        vta = vt_ref[0]
        outs = []
        for h in range(N_HEADS):
            ci = 2 * N_HEADS * d + h
            cf = ci + N_HEADS
            b_row = br[cf:cf + 1, :]
            b_tot = b_row[:, L - 1:L] if d == 0 else b_row[:, 0:1]
            p = h // 2
            qt_pair = qta[p * 2 * ML_DQK:(p + 1) * 2 * ML_DQK, :]
            qtm = jnp.where(pair_row == h % 2, qt_pair, jnp.zeros_like(qt_pair))
            vt_ext = jnp.concatenate([vta[h * ML_DV:(h + 1) * ML_DV, :], ones_rows], axis=0)
            outs.append(_mlstm_head(
                ka[:, p * 2 * ML_DQK:(p + 1) * 2 * ML_DQK], qtm, vt_ext,
                g[:, ci:ci + 1] - bc[:, cf:cf + 1], b_row, gt[ci:ci + 1, :], b_tot, mask_t,
                c_ref[0, d, h], m_ref[0, d, h][:, 0:1]))
        h_ref[0] = jnp.concatenate([o[0] for o in outs], axis=1)
        c_ref[0, d] = jnp.stack([o[1] for o in outs])
        m_ref[0, d] = jnp.stack([jnp.broadcast_to(o[2], (1, LANES)) for o in outs])


def _mlstm(qt, qkc, vt, gates, gates_t, c0, m0):
    bsz, n, _ = qkc.shape
    L = ML_CHUNK
    nc = n // L
    hq = SEC // 2
    fwd = lambda b, i: (b, i, 0)
    bwd = lambda b, i: (b, nc - 1 - i, 0)
    st = lambda b, i: (b, 0, 0, 0, 0)
    c_spec = pl.BlockSpec((1, 2, N_HEADS, 2 * ML_DV, 2 * ML_DQK), st)
    m_spec = pl.BlockSpec((1, 2, N_HEADS, 1, LANES), st)

    def dir_specs(pos):
        return [pl.BlockSpec((1, hq, L), lambda b, i: (b, 0, pos(i))),
                pl.BlockSpec((1, L, hq), lambda b, i: (b, pos(i), 1)),
                pl.BlockSpec((1, SEC, L), lambda b, i: (b, 0, pos(i))),
                pl.BlockSpec((1, L, LANES), lambda b, i: (b, pos(i), 0)),
                pl.BlockSpec((1, 4 * N_HEADS, L), lambda b, i: (b, 0, pos(i)))]

    return pl.pallas_call(
        _mlstm_body,
        grid=(bsz, nc),
        in_specs=dir_specs(lambda i: i) + dir_specs(lambda i: nc - 1 - i) + [c_spec, m_spec],
        out_specs=[pl.BlockSpec((1, L, SEC), fwd), pl.BlockSpec((1, L, SEC), bwd), c_spec, m_spec],
        out_shape=[jax.ShapeDtypeStruct((bsz, n, SEC), F32), jax.ShapeDtypeStruct((bsz, n, SEC), F32),
                   jax.ShapeDtypeStruct(c0.shape, F32), jax.ShapeDtypeStruct(m0.shape, F32)],
        compiler_params=pltpu.CompilerParams(dimension_semantics=("parallel", "arbitrary"),
                                             vmem_limit_bytes=VMEM_LIMIT),
        name="mlstm",
    )(qt, qkc, vt, gates, gates_t, qt, qkc, vt, gates, gates_t, c0, m0)


def _attn_body(lam_ref, q_ref, k_ref, vt_ref, o_ref, qq_s, sa, sb, m_s, acc_s, *, tq, tk, lambda_init):
    nk = k_ref.shape[1] // tk
    nq = q_ref.shape[1] // tq
    blocks_per_iter = 1 if nk % 2 == 0 else 2
    assert nq % blocks_per_iter == 0
    bufs = (sa, sb)
    lm = lam_ref[...]
    lam = (jnp.exp(jnp.sum(lm[0:1] * lm[1:2], axis=-1, keepdims=True))
           - jnp.exp(jnp.sum(lm[2:3] * lm[3:4], axis=-1, keepdims=True)) + lambda_init)

    def scores(blk, t, s_ref):
        if t == 0:
            qoff = pl.multiple_of(jnp.minimum(blk, nq - 1) * tq, LANES)
            qt = jnp.transpose(q_ref[0, pl.ds(qoff, tq), :])
            row = lax.broadcasted_iota(I32, qt.shape, 0)
            zero = jnp.zeros_like(qt)
            qq_s[...] = jnp.concatenate([jnp.where(row < DA_DQK, qt, zero), jnp.where(row >= DA_DQK, qt, zero)],
                                        axis=1)
        s_ref[...] = _dot(k_ref[0, t * tk:(t + 1) * tk, :], qq_s[...])

    def absorb(blk, t, s_ref):
        vts = vt_ref[0, 0, :, t * tk:(t + 1) * tk]
        mt = jnp.max(s_ref[...].reshape(tk // SUBLANES, SUBLANES, 2 * tq).max(axis=0), axis=0, keepdims=True)
        if t == 0:
            m_new = mt
            acc = _dot(vts, jnp.exp2(s_ref[...] - m_new).astype(BF16))
        else:
            m_old = m_s[...]
            m_new = jnp.maximum(m_old, mt)
            p = jnp.exp2(s_ref[...] - m_new)
            acc = jnp.exp2(m_old - m_new) * acc_s[...] + _dot(vts, p.astype(BF16))
        if t == nk - 1:
            o2 = acc[:DA_DV] / acc[DA_DV:DA_DV + 1]
            qoff = pl.multiple_of(blk * tq, LANES)
            o_ref[0, pl.ds(qoff, tq), :] = jnp.transpose(o2[:, :tq] - lam * o2[:, tq:])
        else:
            acc_s[...] = acc
            m_s[...] = m_new

    scores(0, 0, bufs[0])

    def step(j, carry):
        for b in range(blocks_per_iter):
            blk = j * blocks_per_iter + b
            for t in range(nk):
                cur = (b * nk + t) % 2
                if t + 1 < nk:
                    scores(blk, t + 1, bufs[1 - cur])
                else:
                    scores(blk + 1, 0, bufs[1 - cur])
                absorb(blk, t, bufs[cur])
        return carry

    lax.fori_loop(0, nq // blocks_per_iter, step, 0)


def _attn(da_lambda, p_da, k_all, vt, tq, tk, lambda_init):
    bsz, n, _ = p_da.shape
    hd = 2 * DA_DQK
    m = k_all.shape[1]
    return pl.pallas_call(
        functools.partial(_attn_body, tq=tq, tk=tk, lambda_init=lambda_init),
        grid=(bsz, N_HEADS),
        in_specs=[pl.BlockSpec((4, DA_DQK), lambda b, h: (0, 0)),
                  pl.BlockSpec((1, n, hd), lambda b, h: (b, 0, h)),
                  pl.BlockSpec((1, m, hd), lambda b, h: (b, 0, h)),
                  pl.BlockSpec((1, 1, DA_VROWS, m), lambda b, h: (b, h, 0, 0))],
        out_specs=pl.BlockSpec((1, n, DA_DV), lambda b, h: (b, 0, h)),
        out_shape=jax.ShapeDtypeStruct((bsz, n, SEC), F32),
        scratch_shapes=[pltpu.VMEM((hd, 2 * tq), BF16),
                        pltpu.VMEM((tk, 2 * tq), F32), pltpu.VMEM((tk, 2 * tq), F32),
                        pltpu.VMEM((1, 2 * tq), F32), pltpu.VMEM((DA_VROWS, 2 * tq), F32)],
        compiler_params=pltpu.CompilerParams(dimension_semantics=("parallel", "parallel"),
                                             vmem_limit_bytes=VMEM_LIMIT),
        name="diff_attn",
    )(da_lambda, p_da, k_all, vt)


def _merge_body(hf_ref, hb_ref, o_ref, od_ref, x_ref, wout_ref, mlw_ref, daw_ref, g1_ref, ln1w_ref, ln1b_ref,
                sh2_ref, sc2_ref, wr_ref, h1_ref, u_ref, aff_ref, *, lambda_init, n_experts):
    hsum = hf_ref[0] + hb_ref[0]
    og = o_ref[0]
    od = od_ref[0]
    mlw = mlw_ref[...]
    daw = daw_ref[...]
    parts = []
    for h in range(N_HEADS):
        sl = slice(h * ML_DV, (h + 1) * ML_DV)
        parts.append(_ln_rows(hsum[:, sl]) * mlw[:, sl] * _sigmoid(og[:, sl]))
    for h in range(N_HEADS):
        sl = slice(h * DA_DV, (h + 1) * DA_DV)
        z = od[:, sl]
        zn = z * lax.rsqrt(jnp.mean(z * z, axis=-1, keepdims=True) + LN_EPS)
        parts.append(zn * daw[:, sl] * (1.0 - lambda_init))
    ycat = jnp.concatenate(parts, axis=1).astype(BF16)
    mix = _dot(ycat, wout_ref[...])
    h1 = _ln_rows(DEEPNORM_ALPHA * x_ref[0] + g1_ref[0] * mix) * ln1w_ref[...] + ln1b_ref[...]
    h1_ref[0] = h1
    u = _ln_rows(h1) * (1.0 + sc2_ref[0]) + sh2_ref[0]
    u_ref[0] = u
    uhi, umid, _ = _split3(u)
    whi, wmid, _ = _split3(wr_ref[...])
    logits = _dot(uhi, whi) + _dot(uhi, wmid) + _dot(umid, whi)
    lane = lax.broadcasted_iota(I32, logits.shape, 1)
    logits = jnp.where(lane < n_experts, logits, -jnp.inf)
    e = jnp.exp(logits - jnp.max(logits, axis=-1, keepdims=True))
    aff_ref[0] = e / jnp.sum(e, axis=-1, keepdims=True)


def _merge(hf, hb, p_ml, od, x, w_out, ml_norm_w, da_norm_w, g1, ln1_w, ln1_b, sh2, sc2, w_router_pad,
           tm, lambda_init, n_experts):
    bsz, n, d = x.shape
    row = lambda b, i: (b, i, 0)
    vec = lambda b, i: (0, 0)
    mod = lambda b, i: (b, 0, 0)
    return pl.pallas_call(
        functools.partial(_merge_body, lambda_init=lambda_init, n_experts=n_experts),
        grid=(bsz, n // tm),
        in_specs=[pl.BlockSpec((1, tm, SEC), row), pl.BlockSpec((1, tm, SEC), row),
                  pl.BlockSpec((1, tm, SEC), lambda b, i: (b, i, 2)),
                  pl.BlockSpec((1, tm, SEC), row),
                  pl.BlockSpec((1, tm, d), row),
                  pl.BlockSpec((2 * SEC, d), vec),
                  pl.BlockSpec((1, SEC), vec), pl.BlockSpec((1, SEC), vec),
                  pl.BlockSpec((1, 1, d), mod),
                  pl.BlockSpec((1, d), vec), pl.BlockSpec((1, d), vec),
                  pl.BlockSpec((1, 1, d), mod), pl.BlockSpec((1, 1, d), mod),
                  pl.BlockSpec((d, LANES), vec)],
        out_specs=[pl.BlockSpec((1, tm, d), row), pl.BlockSpec((1, tm, d), row),
                   pl.BlockSpec((1, tm, LANES), row)],
        out_shape=[jax.ShapeDtypeStruct((bsz, n, d), F32), jax.ShapeDtypeStruct((bsz, n, d), F32),
                   jax.ShapeDtypeStruct((bsz, n, LANES), F32)],
        compiler_params=pltpu.CompilerParams(dimension_semantics=("parallel", "parallel"),
                                             vmem_limit_bytes=VMEM_LIMIT),
        name="merge_ln1_router",
    )(hf, hb, p_ml, od, x, w_out, ml_norm_w.reshape(1, SEC), da_norm_w.reshape(1, SEC), g1,
      ln1_w.reshape(1, d), ln1_b.reshape(1, d), sh2, sc2, w_router_pad)


def _cumsum_tokens(maskf, uincl_b, lstrict_b):
    mb = maskf.astype(BF16)
    win = _dot(mb, uincl_b)
    tot = jnp.broadcast_to(win[:, LANES - 1:LANES], win.shape)
    offs = _dot(lstrict_b, tot.astype(BF16))
    return offs + win, tot


def _router_body(a_ref, idx_ref, g_ref, *, cap):
    a = a_ref[0]
    rr = a.shape[0]
    bits = lax.bitcast_convert_type(a, I32)
    cur = jnp.zeros((1, 1), I32)
    for bit in range(30, -1, -1):
        cand = cur | (1 << bit)
        cnt = jnp.sum((bits >= cand).astype(I32), keepdims=True)
        cur = jnp.where(cnt >= cap, cand, cur)
    gt = bits > cur
    eq = bits == cur
    need = (cap - jnp.sum(gt.astype(I32), keepdims=True)).astype(F32)

    li = lax.broadcasted_iota(I32, (LANES, LANES), 0)
    lj = lax.broadcasted_iota(I32, (LANES, LANES), 1)
    uincl_b = (li <= lj).astype(BF16)
    ri = lax.broadcasted_iota(I32, (rr, rr), 0)
    rj = lax.broadcasted_iota(I32, (rr, rr), 1)
    lstrict_b = (rj < ri).astype(BF16)
    rincl_b = (ri <= rj).astype(BF16)

    eqf = eq.astype(F32)
    eq_incl, _ = _cumsum_tokens(eqf, uincl_b, lstrict_b)
    sel = gt | (eq & ((eq_incl - eqf) < need))
    self_ = sel.astype(F32)
    csum, _ = _cumsum_tokens(self_, uincl_b, lstrict_b)

    ones8 = jnp.ones((SUBLANES, LANES), BF16)
    tot_lane = lax.dot_general(ones8, self_.astype(BF16), (((1,), (1,)), ((), ())),
                               preferred_element_type=F32)
    end_lane = _dot(tot_lane.astype(BF16), rincl_b)[0:1]
    beg_lane = end_lane - tot_lane[0:1]
    j = lax.broadcasted_iota(I32, (cap, 1), 0).astype(F32)
    onehot = ((beg_lane <= j) & (end_lane > j)).astype(BF16)
    row_idx = jnp.sum((end_lane <= j).astype(F32), axis=-1, keepdims=True)
    chi = jnp.floor(csum * (1.0 / LANES))
    clo = csum - chi * LANES
    crow = _dot(onehot, chi.astype(BF16)) * LANES + _dot(onehot, clo.astype(BF16))
    lane_idx = jnp.sum((crow <= j).astype(F32), axis=-1, keepdims=True)
    idx_ref[0] = (row_idx * LANES + lane_idx).astype(I32)
    ahi, amid, alo = _split3(a)
    arow = _dot(onehot, ahi) + _dot(onehot, amid) + _dot(onehot, alo)
    lane = lax.broadcasted_iota(I32, (cap, LANES), 1).astype(F32)
    g_ref[0] = jnp.sum(jnp.where(lane == lane_idx, arow, 0.0), axis=-1, keepdims=True)


def _router(aff_t, cap):
    g, rr, _ = aff_t.shape
    return pl.pallas_call(
        functools.partial(_router_body, cap=cap),
        grid=(g,),
        in_specs=[pl.BlockSpec((1, rr, LANES), lambda i: (i, 0, 0))],
        out_specs=[pl.BlockSpec((1, cap, 1), lambda i: (i, 0, 0)), pl.BlockSpec((1, cap, 1), lambda i: (i, 0, 0))],
        out_shape=[jax.ShapeDtypeStruct((g, cap, 1), I32), jax.ShapeDtypeStruct((g, cap, 1), F32)],
        compiler_params=pltpu.CompilerParams(dimension_semantics=("parallel",), vmem_limit_bytes=VMEM_LIMIT),
        name="ec_select",
    )(aff_t)


def _moe_body(idx_ref, g_ref, u_hbm, wg_ref, wu_ref, wd_ref, y_in_hbm, y_hbm, buf, xs, tmp, sem,
              *, cap, n_tok, n_experts, rc):
    del y_in_hbm
    grp = pl.program_id(0)
    h = pl.program_id(1)
    base = (grp // n_experts) * n_tok
    d = xs.shape[1]

    @pl.when(h == 0)
    def _():
        def issue(i, carry):
            for r in range(SUBLANES):
                row = base + idx_ref[0, 0, i * SUBLANES + r]
                pltpu.make_async_copy(u_hbm.at[pl.ds(row, 1)], buf.at[i, pl.ds(r, 1)], sem.at[0]).start()
            return carry
        lax.fori_loop(0, cap // SUBLANES, issue, 0)

        def drain(i, carry):
            pltpu.make_async_copy(u_hbm.at[pl.ds(0, SUBLANES)], buf.at[i], sem.at[0]).wait()
            return carry
        lax.fori_loop(0, cap // SUBLANES, drain, 0)
        xs[...] = buf[...].reshape(cap, d).astype(BF16)
        buf[...] = jnp.zeros(buf.shape, F32)

    x = xs[...]
    a = _dot(x, wg_ref[0].astype(BF16))
    b = _dot(x, wu_ref[0].astype(BF16))
    hid = (a * _sigmoid(a) * b).astype(BF16)
    buf[...] += _dot(hid, wd_ref[0].astype(BF16)).reshape(buf.shape)

    def y_rows(c, slot, to_hbm):
        def issue(i, carry):
            for r in range(SUBLANES):
                row = base + idx_ref[0, 0, c * rc + i * SUBLANES + r]
                hbm_row = y_hbm.at[pl.ds(row, 1)]
                vmem_row = tmp.at[slot, i, pl.ds(r, 1)]
                if to_hbm:
                    pltpu.make_async_copy(vmem_row, hbm_row, sem.at[3 + slot]).start()
                else:
                    pltpu.make_async_copy(hbm_row, vmem_row, sem.at[1 + slot]).start()
            return carry
        lax.fori_loop(0, rc // SUBLANES, issue, 0)

    def y_wait(slot, to_hbm):
        def drain(i, carry):
            hbm_rows = y_hbm.at[pl.ds(0, SUBLANES)]
            if to_hbm:
                pltpu.make_async_copy(tmp.at[slot, i], hbm_rows, sem.at[3 + slot]).wait()
            else:
                pltpu.make_async_copy(hbm_rows, tmp.at[slot, i], sem.at[1 + slot]).wait()
            return carry
        lax.fori_loop(0, rc // SUBLANES, drain, 0)

    @pl.when(h == pl.num_programs(1) - 1)
    def _():
        n_chunks = cap // rc
        y_rows(0, 0, False)
        for c in range(n_chunks):
            slot = c % 2
            if c + 1 < n_chunks:
                if c >= 1:
                    y_wait(1 - slot, True)
                y_rows(c + 1, 1 - slot, False)
            y_wait(slot, False)
            acc = buf[c * rc // SUBLANES:(c + 1) * rc // SUBLANES].reshape(rc, d)
            tmp[slot] = tmp[slot] + (g_ref[0, c * rc:(c + 1) * rc, :] * acc).reshape(tmp.shape[1:])
            y_rows(c, slot, True)
        for c in range(max(n_chunks - 2, 0), n_chunks):
            y_wait(c % 2, True)


def _moe(idx3, gsel, u2, w_gate, w_up, w_down, y0, n_tok, th):
    g, _, cap = idx3.shape
    n_experts, d, de = w_gate.shape
    rc = min(256, max(cap // 4, SUBLANES))
    return pl.pallas_call(
        functools.partial(_moe_body, cap=cap, n_tok=n_tok, n_experts=n_experts, rc=rc),
        grid=(g, de // th),
        in_specs=[pl.BlockSpec((1, 1, cap), lambda i, h: (i, 0, 0), memory_space=pltpu.SMEM),
                  pl.BlockSpec((1, cap, 1), lambda i, h: (i, 0, 0)),
                  pl.BlockSpec(memory_space=pl.ANY),
                  pl.BlockSpec((1, d, th), lambda i, h: (i % n_experts, 0, h)),
                  pl.BlockSpec((1, d, th), lambda i, h: (i % n_experts, 0, h)),
                  pl.BlockSpec((1, th, d), lambda i, h: (i % n_experts, h, 0)),
                  pl.BlockSpec(memory_space=pl.ANY)],
        out_specs=pl.BlockSpec(memory_space=pl.ANY),
        out_shape=jax.ShapeDtypeStruct(y0.shape, F32),
        scratch_shapes=[pltpu.VMEM((cap // SUBLANES, SUBLANES, d), F32), pltpu.VMEM((cap, d), BF16),
                        pltpu.VMEM((2, rc // SUBLANES, SUBLANES, d), F32),
                        pltpu.SemaphoreType.DMA((5,))],
        input_output_aliases={6: 0},
        compiler_params=pltpu.CompilerParams(dimension_semantics=("arbitrary", "arbitrary"),
                                             vmem_limit_bytes=VMEM_LIMIT, disable_bounds_checks=True),
        name="ec_moe",
    )(idx3, gsel, u2, w_gate, w_up, w_down, y0)


def _final_body(h1_ref, y_ref, g2_ref, w_ref, b_ref, o_ref):
    t = DEEPNORM_ALPHA * h1_ref[0] + g2_ref[0] * y_ref[0]
    o_ref[0] = _ln_rows(t) * w_ref[...] + b_ref[...]


def _final(h1, y, g2, ln2_w, ln2_b, tm):
    bsz, n, d = h1.shape
    row = lambda b, i: (b, i, 0)
    return pl.pallas_call(
        _final_body,
        grid=(bsz, n // tm),
        in_specs=[pl.BlockSpec((1, tm, d), row), pl.BlockSpec((1, tm, d), row),
                  pl.BlockSpec((1, 1, d), lambda b, i: (b, 0, 0)),
                  pl.BlockSpec((1, d), lambda b, i: (0, 0)), pl.BlockSpec((1, d), lambda b, i: (0, 0))],
        out_specs=pl.BlockSpec((1, tm, d), row),
        out_shape=jax.ShapeDtypeStruct((bsz, n, d), F32),
        compiler_params=pltpu.CompilerParams(dimension_semantics=("parallel", "parallel"),
                                             vmem_limit_bytes=VMEM_LIMIT),
        name="ln2",
    )(h1, y, g2, ln2_w.reshape(1, d), ln2_b.reshape(1, d))


def _rope_tables(n):
    rows = n // GRID_W
    pos_row = jnp.repeat(jnp.arange(rows, dtype=I32), GRID_W).astype(F32)
    pos_col = jnp.tile(jnp.arange(GRID_W, dtype=I32), rows).astype(F32)
    n_freq = DA_DQK // 4
    inv = ROPE_BASE ** (-jnp.arange(n_freq, dtype=F32) / n_freq)
    ang = jnp.concatenate([pos_row[:, None] * inv, pos_col[:, None] * inv], -1)
    cos, sin = jnp.cos(ang), jnp.sin(ang)
    reps = LANES // DA_DQK
    cosf = jnp.tile(cos, (1, 2 * reps))
    sins = jnp.tile(jnp.concatenate([-sin, sin], -1), (1, reps))
    return cosf, sins


def _pick(n, prefs):
    for t in prefs:
        if n % t == 0:
            return t
    return n


def kernel(x, c, ctx, c_ctx, w_ada, b_ada, w_in, b_gates, ml_conv_w, ml_conv_b, ml_norm_w, da_lambda, da_norm_w,
           w_out, ln1_w, ln1_b, w_router, w_gate, w_up, w_down, ln2_w, ln2_b):
    bsz, n, d = x.shape
    n_ctx = ctx.shape[1]
    n_experts = w_gate.shape[1]
    cap = EC_CAPACITY_FACTOR * n // n_experts
    l = 0
    lambda_init = 0.8 - 0.6 * math.exp(-0.3 * l)

    cin = jnp.zeros((SUBLANES, d), F32).at[:bsz].set(c).at[bsz].set(c_ctx)
    mod = _ada(cin, w_ada[l], b_ada[l])
    sh1, sc1, g1, sh2, sc2, g2 = (mod[:bsz, k * d:(k + 1) * d][:, None, :] for k in range(6))
    csh1, csc1 = (mod[bsz:bsz + 1, k * d:(k + 1) * d][:, None, :] for k in range(2))

    w = w_in[l]
    off_gates = 3 * SEC
    n_gate = 4 * N_HEADS
    w_main = jnp.concatenate([w[:, :off_gates], w[:, off_gates + n_gate:]], axis=1).astype(BF16)
    w_g = jnp.pad(w[:, off_gates:off_gates + n_gate], ((0, 0), (0, LANES - n_gate))).astype(BF16)
    b_g = jnp.pad(b_gates[l], (0, LANES - n_gate)).reshape(1, LANES)
    pml_l, pda_l, gates_l = _proj(x, sh1, sc1, w_main, w_g, b_g, _rope_tables(n), _pick(n, (512, 256, 128)))
    pml_c, pda_c, gates_c = _proj(ctx, csh1, csc1, w_main, w_g, b_g, None, _pick(n_ctx, (512, 256, 128)))

    qkc_l = _conv(pml_l, ml_conv_w[l], ml_conv_b[l], _pick(n, (512, 256, 128)))
    qkc_c = _conv(pml_c, ml_conv_w[l], ml_conv_b[l], _pick(n_ctx, (512, 256, 128)))
    gt_l = jnp.swapaxes(gates_l[:, :, :n_gate], 1, 2)
    gt_c = jnp.swapaxes(gates_c[:, :, :n_gate], 1, 2)
    qt_l = jnp.swapaxes(qkc_l[:, :, :SEC // 2], 1, 2)
    qt_c = jnp.swapaxes(qkc_c[:, :, :SEC // 2], 1, 2)
    vt_l = jnp.swapaxes(pml_l[:, :, SEC:2 * SEC].astype(BF16), 1, 2)
    vt_c = jnp.swapaxes(pml_c[:, :, SEC:2 * SEC].astype(BF16), 1, 2)
    c0 = jnp.zeros((bsz, 2, N_HEADS, 2 * ML_DV, 2 * ML_DQK), F32)
    m0 = jnp.zeros((bsz, 2, N_HEADS, 1, LANES), F32)
    _, _, c1, m1 = _mlstm(qt_c, qkc_c, vt_c, gates_c, gt_c, c0, m0)
    hf, hb, _, _ = _mlstm(qt_l, qkc_l, vt_l, gates_l, gt_l, c1, m1)

    m_all = n + n_ctx
    k_all = jnp.concatenate([pda_l[:, :, SEC:2 * SEC], pda_c[:, :, SEC:2 * SEC]], axis=1)
    v_all = jnp.concatenate([pda_l[:, :, 2 * SEC:], pda_c[:, :, 2 * SEC:]], axis=1)
    vt = v_all.reshape(bsz, m_all, N_HEADS, DA_DV).transpose(0, 2, 3, 1)
    ones_rows = jnp.zeros((bsz, N_HEADS, DA_VROWS - DA_DV, m_all), BF16).at[:, :, 0].set(1.0)
    vt = jnp.concatenate([vt, ones_rows], axis=2)
    od = _attn(da_lambda[l], pda_l, k_all, vt, _pick(n, (256, 128)), _pick(m_all, (1280, 256, 128)), lambda_init)

    w_router_pad = jnp.pad(w_router[l], ((0, 0), (0, LANES - n_experts)))
    h1, u, aff = _merge(hf, hb, pml_l, od, x, w_out[l].astype(BF16), ml_norm_w[l], da_norm_w[l], g1,
                        ln1_w[l], ln1_b[l], sh2, sc2, w_router_pad, _pick(n, (256, 128)), lambda_init, n_experts)

    aff_t = jnp.swapaxes(aff[:, :, :n_experts], 1, 2).reshape(bsz * n_experts, n // LANES, LANES)
    idx, gsel = _router(aff_t, cap)
    y = _moe(idx.reshape(bsz * n_experts, 1, cap), gsel, u.reshape(bsz * n, d), w_gate[l], w_up[l], w_down[l],
             jnp.zeros((bsz * n, d), F32), n, _pick(w_gate.shape[3], (256, 128)))
    return _final(h1, y.reshape(bsz, n, d), g2, ln2_w[l], ln2_b[l], _pick(n, (512, 256, 128)))
```

```python
import functools
import math

import jax
import jax.numpy as jnp
from jax import lax
from jax.experimental import pallas as pl
from jax.experimental.pallas import tpu as pltpu

F32 = jnp.float32
BF16 = jnp.bfloat16
I32 = jnp.int32

LANES = 128
SUBLANES = 8
GRID_W = 64
N_HEADS = 8
ML_DQK = 64
ML_DV = 128
ML_CONV_W = 5
ML_CHUNK = 128
DA_DQK = 64
DA_DV = 128
SEC = 1024
ROPE_BASE = 10000.0
LN_EPS = 1e-5
DEPTH = 1
DEEPNORM_ALPHA = (2.0 * DEPTH) ** 0.25
EC_CAPACITY_FACTOR = 2
VMEM_LIMIT = 56 * 1024 * 1024
DA_QSCALE = DA_DQK ** -0.5 * math.log2(math.e)
DA_VROWS = DA_DV + 16


def _sigmoid(x):
    return 1.0 / (1.0 + jnp.exp(-x))


def _log_sigmoid(x):
    return jnp.minimum(x, 0.0) - jnp.log(1.0 + jnp.exp(-jnp.abs(x)))


def _ln_rows(x):
    mu = jnp.mean(x, axis=-1, keepdims=True)
    xc = x - mu
    var = jnp.mean(xc * xc, axis=-1, keepdims=True)
    return xc * lax.rsqrt(var + LN_EPS)


def _split3(x):
    hi = x.astype(BF16)
    r1 = x - hi.astype(F32)
    mid = r1.astype(BF16)
    lo = (r1 - mid.astype(F32)).astype(BF16)
    return hi, mid, lo


def _dot(a, b):
    return jnp.dot(a, b, preferred_element_type=F32)


def _dot_exact_lhs(a_bf16, x_f32):
    hi, mid, lo = _split3(x_f32)
    return _dot(a_bf16, hi) + _dot(a_bf16, mid) + _dot(a_bf16, lo)


def _dot_exact_rhs(x_f32, a_bf16):
    hi, mid, lo = _split3(x_f32)
    return _dot(hi, a_bf16) + _dot(mid, a_bf16) + _dot(lo, a_bf16)


def _ada_body(c_ref, w_ref, b_ref, o_ref):
    c = c_ref[...]
    a = c * _sigmoid(c)
    hi, mid, lo = _split3(a)
    w = w_ref[...]
    whi, wmid, wlo = _split3(w)
    acc = _dot(hi, whi) + _dot(hi, wmid) + _dot(mid, whi)
    acc = acc + _dot(mid, wmid) + _dot(hi, wlo) + _dot(lo, whi)
    o_ref[...] = acc + b_ref[...]


def _ada(cin, w, b):
    rows, d = cin.shape
    n = w.shape[1]
    tn = _pick(n, (1024, 512, 256, 128))
    return pl.pallas_call(
        _ada_body,
        grid=(n // tn,),
        in_specs=[pl.BlockSpec((rows, d), lambda j: (0, 0)),
                  pl.BlockSpec((d, tn), lambda j: (0, j)),
                  pl.BlockSpec((1, tn), lambda j: (0, j))],
        out_specs=pl.BlockSpec((rows, tn), lambda j: (0, j)),
        out_shape=jax.ShapeDtypeStruct((rows, n), F32),
        compiler_params=pltpu.CompilerParams(dimension_semantics=("arbitrary",),
                                             vmem_limit_bytes=VMEM_LIMIT),
        name="ada_mod",
    )(cin, w, b.reshape(1, n))


def _rope_tile(acc, cosf, sins):
    outs = []
    lane = lax.broadcasted_iota(I32, (acc.shape[0], LANES), 1)
    first = (lane & (DA_DQK - 1)) < (DA_DQK // 2)
    for t in range(acc.shape[1] // LANES):
        a = acc[:, t * LANES:(t + 1) * LANES]
        partner = jnp.where(first, pltpu.roll(a, LANES - DA_DQK // 2, 1), pltpu.roll(a, DA_DQK // 2, 1))
        outs.append(a * cosf + partner * sins)
    return jnp.concatenate(outs, axis=1)


def _proj_body(*refs, rope):
    if rope:
        (x_ref, sh_ref, sc_ref, w_ref, wg_ref, bg_ref, cos_ref, sin_ref,
         pml_ref, pda_ref, g_ref, xn_ref) = refs
    else:
        (x_ref, sh_ref, sc_ref, w_ref, wg_ref, bg_ref, pml_ref, pda_ref, g_ref, xn_ref) = refs
    j = pl.program_id(2)

    @pl.when(j == 0)
    def _():
        y = _ln_rows(x_ref[0]) * (1.0 + sc_ref[0]) + sh_ref[0]
        yb = y.astype(BF16)
        xn_ref[...] = yb
        g_ref[0] = _dot(yb, wg_ref[...]) + bg_ref[...]

    acc = _dot(xn_ref[...], w_ref[...])

    @pl.when(j < 3)
    def _():
        pml_ref[0] = acc

    @pl.when(j == 3)
    def _():
        q = _rope_tile(acc, cos_ref[...], sin_ref[...]) if rope else acc
        pda_ref[0] = (q * DA_QSCALE).astype(BF16)

    @pl.when(j == 4)
    def _():
        k = _rope_tile(acc, cos_ref[...], sin_ref[...]) if rope else acc
        pda_ref[0] = k.astype(BF16)

    @pl.when(j == 5)
    def _():
        pda_ref[0] = acc.astype(BF16)


def _proj(x, sh, sc, w_main, w_g, b_g, rope_tabs, tm):
    bsz, n, d = x.shape
    per_batch = sh.shape[0] == bsz
    rope = rope_tabs is not None
    mod_map = (lambda b, i, j: (b, 0, 0)) if per_batch else (lambda b, i, j: (0, 0, 0))
    in_specs = [pl.BlockSpec((1, tm, d), lambda b, i, j: (b, i, 0)),
                pl.BlockSpec((1, 1, d), mod_map),
                pl.BlockSpec((1, 1, d), mod_map),
                pl.BlockSpec((d, SEC), lambda b, i, j: (0, j)),
                pl.BlockSpec((d, LANES), lambda b, i, j: (0, 0)),
                pl.BlockSpec((1, LANES), lambda b, i, j: (0, 0))]
    args = [x, sh, sc, w_main, w_g, b_g]
    if rope:
        in_specs += [pl.BlockSpec((tm, LANES), lambda b, i, j: (i, 0)),
                     pl.BlockSpec((tm, LANES), lambda b, i, j: (i, 0))]
        args += list(rope_tabs)
    return pl.pallas_call(
        functools.partial(_proj_body, rope=rope),
        grid=(bsz, n // tm, 6),
        in_specs=in_specs,
        out_specs=[pl.BlockSpec((1, tm, SEC), lambda b, i, j: (b, i, jnp.minimum(j, 2))),
                   pl.BlockSpec((1, tm, SEC), lambda b, i, j: (b, i, jnp.maximum(j - 3, 0))),
                   pl.BlockSpec((1, tm, LANES), lambda b, i, j: (b, i, 0))],
        out_shape=[jax.ShapeDtypeStruct((bsz, n, 3 * SEC), F32),
                   jax.ShapeDtypeStruct((bsz, n, 3 * SEC), BF16),
                   jax.ShapeDtypeStruct((bsz, n, LANES), F32)],
        scratch_shapes=[pltpu.VMEM((tm, d), BF16)],
        compiler_params=pltpu.CompilerParams(dimension_semantics=("parallel", "parallel", "arbitrary"),
                                             vmem_limit_bytes=VMEM_LIMIT),
        name="in_proj_rope" if rope else "in_proj",
    )(*args)


def _conv_body(prev_ref, cur_ref, next_ref, w_ref, b_ref, s_ref, o_ref):
    i = pl.program_id(1)
    last = pl.num_programs(1) - 1
    cur = cur_ref[0]
    t = cur.shape[0]
    prev = jnp.where(i > 0, prev_ref[0], 0.0)
    nxt = jnp.where(i < last, next_ref[0], 0.0)
    ext = jnp.concatenate([prev, cur, nxt], axis=0)
    acc = jnp.zeros_like(cur) + b_ref[...]
    half = ML_CONV_W // 2
    for k in range(ML_CONV_W):
        off = SUBLANES - half + k
        acc = acc + ext[off:off + t] * w_ref[k:k + 1, :]
    y = acc * _sigmoid(acc)
    o_ref[0] = (y * s_ref[...]).astype(BF16)


def _conv(p_ml, conv_w, conv_b, tt):
    bsz, n, _ = p_ml.shape
    nb8 = n // SUBLANES
    r = tt // SUBLANES
    scale = jnp.concatenate([jnp.full((1, SEC // 2), ML_DQK ** -0.5, F32), jnp.ones((1, SEC // 2), F32)], axis=1)
    return pl.pallas_call(
        _conv_body,
        grid=(bsz, n // tt),
        in_specs=[pl.BlockSpec((1, SUBLANES, SEC), lambda b, i: (b, jnp.maximum(i * r - 1, 0), 0)),
                  pl.BlockSpec((1, tt, SEC), lambda b, i: (b, i, 0)),
                  pl.BlockSpec((1, SUBLANES, SEC), lambda b, i: (b, jnp.minimum((i + 1) * r, nb8 - 1), 0)),
                  pl.BlockSpec((ML_CONV_W, SEC), lambda b, i: (0, 0)),
                  pl.BlockSpec((1, SEC), lambda b, i: (0, 0)),
                  pl.BlockSpec((1, SEC), lambda b, i: (0, 0))],
        out_specs=pl.BlockSpec((1, tt, SEC), lambda b, i: (b, i, 0)),
        out_shape=jax.ShapeDtypeStruct((bsz, n, SEC), BF16),
        compiler_params=pltpu.CompilerParams(dimension_semantics=("parallel", "parallel"),
                                             vmem_limit_bytes=VMEM_LIMIT),
        name="ml_conv",
    )(p_ml, p_ml, p_ml, conv_w, conv_b.reshape(1, SEC), scale)


def _mlstm_head(kp, qtm, vt_ext, r_col, b_row, i_row, b_tot, mask_t, c_prev, m_prev):
    dmat = jnp.where(mask_t, b_row + r_col, -jnp.inf)
    m_inter = b_row + m_prev
    m_t = jnp.maximum(m_inter, jnp.max(dmat, axis=0, keepdims=True))
    w_inter = jnp.exp(m_inter - m_t)
    w = jnp.exp(dmat - m_t) * _dot(kp, qtm)
    ext = w_inter * _dot(c_prev.astype(BF16), qtm) + _dot(vt_ext, w.astype(BF16))
    ht = ext[:ML_DV] / jnp.maximum(jnp.abs(ext[ML_DV:ML_DV + 1]), jnp.exp(-m_t))
    gdec = b_tot - b_row + i_row
    m_new = jnp.maximum(b_tot + m_prev, jnp.max(gdec, axis=-1, keepdims=True))
    a = jnp.exp(b_tot + m_prev - m_new)
    vw = (vt_ext.astype(F32) * jnp.exp(gdec - m_new)).astype(BF16)
    c_new = a * c_prev + _dot(vw, kp)
    return jnp.transpose(ht), c_new, m_new


def _mlstm_body(qf_ref, kf_ref, vf_ref, gf_ref, gtf_ref, qb_ref, kb_ref, vb_ref, gb_ref, gtb_ref,
                c0_ref, m0_ref, hf_ref, hb_ref, c_ref, m_ref):
    @pl.when(pl.program_id(1) == 0)
    def _():
        c_ref[...] = c0_ref[...]
        m_ref[...] = m0_ref[...]

    L = ML_CHUNK
    r = lax.broadcasted_iota(I32, (L, L), 0)
    s = lax.broadcasted_iota(I32, (L, L), 1)
    mle = r <= s
    mge = r >= s
    mle_b = mle.astype(BF16)
    mge_b = mge.astype(BF16)
    ones_rows = (lax.broadcasted_iota(I32, (ML_DV, L), 0) == 0).astype(BF16)
    pair_row = lax.broadcasted_iota(I32, (2 * ML_DQK, L), 0) // ML_DQK
    dirs = ((qf_ref, kf_ref, vf_ref, gf_ref, gtf_ref, hf_ref), (qb_ref, kb_ref, vb_ref, gb_ref, gtb_ref, hb_ref))
    for d, (q_ref, k_ref, v_ref, g_ref, gt_ref, h_ref) in enumerate(dirs):
        g = g_ref[0]
        gt = gt_ref[0]
        bc = _dot_exact_lhs(mge_b if d == 0 else mle_b, _log_sigmoid(g))
        br = _dot_exact_rhs(_log_sigmoid(gt), mle_b if d == 0 else mge_b)
        mask_t = mle if d == 0 else mge
        qta = jnp.transpose(q_ref[0])
        ka = k_ref[0]
        vta = jnp.transpose(v_ref[0].astype(BF16))
        outs = []
        for h in range(N_HEADS):
            ci = 2 * N_HEADS * d + h
            cf = ci + N_HEADS
            b_row = br[cf:cf + 1, :]
            b_tot = b_row[:, L - 1:L] if d == 0 else b_row[:, 0:1]
            p = h // 2
            qt_pair = qta[p * 2 * ML_DQK:(p + 1) * 2 * ML_DQK, :]
            qtm = jnp.where(pair_row == h % 2, qt_pair, jnp.zeros_like(qt_pair))
            vt_ext = jnp.concatenate([vta[h * ML_DV:(h + 1) * ML_DV, :], ones_rows], axis=0)
            outs.append(_mlstm_head(
                ka[:, p * 2 * ML_DQK:(p + 1) * 2 * ML_DQK], qtm, vt_ext,
                g[:, ci:ci + 1] - bc[:, cf:cf + 1], b_row, gt[ci:ci + 1, :], b_tot, mask_t,
                c_ref[0, d, h], m_ref[0, d, h][:, 0:1]))
        h_ref[0] = jnp.concatenate([o[0] for o in outs], axis=1)
        c_ref[0, d] = jnp.stack([o[1] for o in outs])
        m_ref[0, d] = jnp.stack([jnp.broadcast_to(o[2], (1, LANES)) for o in outs])


def _mlstm(qkc, p_ml, gates, gates_t, c0, m0):
    bsz, n, _ = qkc.shape
    L = ML_CHUNK
    nc = n // L
    hq = SEC // 2
    fwd = lambda b, i: (b, i, 0)
    bwd = lambda b, i: (b, nc - 1 - i, 0)
    st = lambda b, i: (b, 0, 0, 0, 0)
    c_spec = pl.BlockSpec((1, 2, N_HEADS, 2 * ML_DV, 2 * ML_DQK), st)
    m_spec = pl.BlockSpec((1, 2, N_HEADS, 1, LANES), st)

    def dir_specs(pos):
        return [pl.BlockSpec((1, L, hq), lambda b, i: (b, pos(i), 0)),
                pl.BlockSpec((1, L, hq), lambda b, i: (b, pos(i), 1)),
                pl.BlockSpec((1, L, SEC), lambda b, i: (b, pos(i), 1)),
                pl.BlockSpec((1, L, LANES), lambda b, i: (b, pos(i), 0)),
                pl.BlockSpec((1, 4 * N_HEADS, L), lambda b, i: (b, 0, pos(i)))]

    return pl.pallas_call(
        _mlstm_body,
        grid=(bsz, nc),
        in_specs=dir_specs(lambda i: i) + dir_specs(lambda i: nc - 1 - i) + [c_spec, m_spec],
        out_specs=[pl.BlockSpec((1, L, SEC), fwd), pl.BlockSpec((1, L, SEC), bwd), c_spec, m_spec],
        out_shape=[jax.ShapeDtypeStruct((bsz, n, SEC), F32), jax.ShapeDtypeStruct((bsz, n, SEC), F32),
                   jax.ShapeDtypeStruct(c0.shape, F32), jax.ShapeDtypeStruct(m0.shape, F32)],
        compiler_params=pltpu.CompilerParams(dimension_semantics=("parallel", "arbitrary"),
                                             vmem_limit_bytes=VMEM_LIMIT),
        name="mlstm",
    )(qkc, qkc, p_ml, gates, gates_t, qkc, qkc, p_ml, gates, gates_t, c0, m0)


def _attn_body(lam_ref, q_ref, k_ref, vt_ref, o_ref, qq_s, sa, sb, m_s, acc_s, *, tq, tk, lambda_init):
    nk = k_ref.shape[1] // tk
    nq = q_ref.shape[1] // tq
    blocks_per_iter = 1 if nk % 2 == 0 else 2
    assert nq % blocks_per_iter == 0
    bufs = (sa, sb)
    lm = lam_ref[...]
    lam = (jnp.exp(jnp.sum(lm[0:1] * lm[1:2], axis=-1, keepdims=True))
           - jnp.exp(jnp.sum(lm[2:3] * lm[3:4], axis=-1, keepdims=True)) + lambda_init)

    def scores(blk, t, s_ref):
        if t == 0:
            qoff = pl.multiple_of(jnp.minimum(blk, nq - 1) * tq, LANES)
            qt = jnp.transpose(q_ref[0, pl.ds(qoff, tq), :])
            row = lax.broadcasted_iota(I32, qt.shape, 0)
            zero = jnp.zeros_like(qt)
            qq_s[...] = jnp.concatenate([jnp.where(row < DA_DQK, qt, zero), jnp.where(row >= DA_DQK, qt, zero)],
                                        axis=1)
        s_ref[...] = _dot(k_ref[0, t * tk:(t + 1) * tk, :], qq_s[...])

    def absorb(blk, t, s_ref):
        vts = vt_ref[0, 0, :, t * tk:(t + 1) * tk]
        mt = jnp.max(s_ref[...].reshape(tk // SUBLANES, SUBLANES, 2 * tq).max(axis=0), axis=0, keepdims=True)
        if t == 0:
            m_new = mt
            acc = _dot(vts, jnp.exp2(s_ref[...] - m_new).astype(BF16))
        else:
            m_old = m_s[...]
            m_new = jnp.maximum(m_old, mt)
            p = jnp.exp2(s_ref[...] - m_new)
            acc = jnp.exp2(m_old - m_new) * acc_s[...] + _dot(vts, p.astype(BF16))
        if t == nk - 1:
            o2 = acc[:DA_DV] / acc[DA_DV:DA_DV + 1]
            qoff = pl.multiple_of(blk * tq, LANES)
            o_ref[0, pl.ds(qoff, tq), :] = jnp.transpose(o2[:, :tq] - lam * o2[:, tq:])
        else:
            acc_s[...] = acc
            m_s[...] = m_new

    scores(0, 0, bufs[0])

    def step(j, carry):
        for b in range(blocks_per_iter):
            blk = j * blocks_per_iter + b
            for t in range(nk):
                cur = (b * nk + t) % 2
                if t + 1 < nk:
                    scores(blk, t + 1, bufs[1 - cur])
                else:
                    scores(blk + 1, 0, bufs[1 - cur])
                absorb(blk, t, bufs[cur])
        return carry

    lax.fori_loop(0, nq // blocks_per_iter, step, 0)


def _attn(da_lambda, p_da, k_all, vt, tq, tk, lambda_init):
    bsz, n, _ = p_da.shape
    hd = 2 * DA_DQK
    m = k_all.shape[1]
    return pl.pallas_call(
        functools.partial(_attn_body, tq=tq, tk=tk, lambda_init=lambda_init),
        grid=(bsz, N_HEADS),
        in_specs=[pl.BlockSpec((4, DA_DQK), lambda b, h: (0, 0)),
                  pl.BlockSpec((1, n, hd), lambda b, h: (b, 0, h)),
                  pl.BlockSpec((1, m, hd), lambda b, h: (b, 0, h)),
                  pl.BlockSpec((1, 1, DA_VROWS, m), lambda b, h: (b, h, 0, 0))],
        out_specs=pl.BlockSpec((1, n, DA_DV), lambda b, h: (b, 0, h)),
        out_shape=jax.ShapeDtypeStruct((bsz, n, SEC), F32),
        scratch_shapes=[pltpu.VMEM((hd, 2 * tq), BF16),
                        pltpu.VMEM((tk, 2 * tq), F32), pltpu.VMEM((tk, 2 * tq), F32),
                        pltpu.VMEM((1, 2 * tq), F32), pltpu.VMEM((DA_VROWS, 2 * tq), F32)],
        compiler_params=pltpu.CompilerParams(dimension_semantics=("parallel", "parallel"),
                                             vmem_limit_bytes=VMEM_LIMIT),
        name="diff_attn",
    )(da_lambda, p_da, k_all, vt)


def _merge_body(hf_ref, hb_ref, o_ref, od_ref, x_ref, wout_ref, mlw_ref, daw_ref, g1_ref, ln1w_ref, ln1b_ref,
                sh2_ref, sc2_ref, wr_ref, h1_ref, u_ref, aff_ref, *, lambda_init, n_experts):
    hsum = hf_ref[0] + hb_ref[0]
    og = o_ref[0]
    od = od_ref[0]
    mlw = mlw_ref[...]
    daw = daw_ref[...]
    parts = []
    for h in range(N_HEADS):
        sl = slice(h * ML_DV, (h + 1) * ML_DV)
        parts.append(_ln_rows(hsum[:, sl]) * mlw[:, sl] * _sigmoid(og[:, sl]))
    for h in range(N_HEADS):
        sl = slice(h * DA_DV, (h + 1) * DA_DV)
        z = od[:, sl]
        zn = z * lax.rsqrt(jnp.mean(z * z, axis=-1, keepdims=True) + LN_EPS)
        parts.append(zn * daw[:, sl] * (1.0 - lambda_init))
    ycat = jnp.concatenate(parts, axis=1).astype(BF16)
    mix = _dot(ycat, wout_ref[...])
    h1 = _ln_rows(DEEPNORM_ALPHA * x_ref[0] + g1_ref[0] * mix) * ln1w_ref[...] + ln1b_ref[...]
    h1_ref[0] = h1
    u = _ln_rows(h1) * (1.0 + sc2_ref[0]) + sh2_ref[0]
    u_ref[0] = u
    uhi, umid, _ = _split3(u)
    whi, wmid, _ = _split3(wr_ref[...])
    logits = _dot(uhi, whi) + _dot(uhi, wmid) + _dot(umid, whi)
    lane = lax.broadcasted_iota(I32, logits.shape, 1)
    logits = jnp.where(lane < n_experts, logits, -jnp.inf)
    e = jnp.exp(logits - jnp.max(logits, axis=-1, keepdims=True))
    aff_ref[0] = e / jnp.sum(e, axis=-1, keepdims=True)


def _merge(hf, hb, p_ml, od, x, w_out, ml_norm_w, da_norm_w, g1, ln1_w, ln1_b, sh2, sc2, w_router_pad,
           tm, lambda_init, n_experts):
    bsz, n, d = x.shape
    row = lambda b, i: (b, i, 0)
    vec = lambda b, i: (0, 0)
    mod = lambda b, i: (b, 0, 0)
    return pl.pallas_call(
        functools.partial(_merge_body, lambda_init=lambda_init, n_experts=n_experts),
        grid=(bsz, n // tm),
        in_specs=[pl.BlockSpec((1, tm, SEC), row), pl.BlockSpec((1, tm, SEC), row),
                  pl.BlockSpec((1, tm, SEC), lambda b, i: (b, i, 2)),
                  pl.BlockSpec((1, tm, SEC), row),
                  pl.BlockSpec((1, tm, d), row),
                  pl.BlockSpec((2 * SEC, d), vec),
                  pl.BlockSpec((1, SEC), vec), pl.BlockSpec((1, SEC), vec),
                  pl.BlockSpec((1, 1, d), mod),
                  pl.BlockSpec((1, d), vec), pl.BlockSpec((1, d), vec),
                  pl.BlockSpec((1, 1, d), mod), pl.BlockSpec((1, 1, d), mod),
                  pl.BlockSpec((d, LANES), vec)],
        out_specs=[pl.BlockSpec((1, tm, d), row), pl.BlockSpec((1, tm, d), row),
                   pl.BlockSpec((1, tm, LANES), row)],
        out_shape=[jax.ShapeDtypeStruct((bsz, n, d), F32), jax.ShapeDtypeStruct((bsz, n, d), F32),
                   jax.ShapeDtypeStruct((bsz, n, LANES), F32)],
        compiler_params=pltpu.CompilerParams(dimension_semantics=("parallel", "parallel"),
                                             vmem_limit_bytes=VMEM_LIMIT),
        name="merge_ln1_router",
    )(hf, hb, p_ml, od, x, w_out, ml_norm_w.reshape(1, SEC), da_norm_w.reshape(1, SEC), g1,
      ln1_w.reshape(1, d), ln1_b.reshape(1, d), sh2, sc2, w_router_pad)


def _cumsum_tokens(maskf, uincl_b, lstrict_b):
    mb = maskf.astype(BF16)
    win = _dot(mb, uincl_b)
    tot = jnp.broadcast_to(win[:, LANES - 1:LANES], win.shape)
    offs = _dot(lstrict_b, tot.astype(BF16))
    return offs + win, tot


def _router_body(a_ref, idx_ref, g_ref, *, cap):
    a = a_ref[0]
    rr = a.shape[0]
    cur = jnp.zeros((1, 1), I32)
    for bit in range(30, -1, -1):
        cand = cur | (1 << bit)
        cnt = jnp.sum((a >= lax.bitcast_convert_type(cand, F32)).astype(I32), keepdims=True)
        cur = jnp.where(cnt >= cap, cand, cur)
    thr = lax.bitcast_convert_type(cur, F32)
    gt = a > thr
    eq = a == thr
    need = (cap - jnp.sum(gt.astype(I32), keepdims=True)).astype(F32)

    li = lax.broadcasted_iota(I32, (LANES, LANES), 0)
    lj = lax.broadcasted_iota(I32, (LANES, LANES), 1)
    uincl_b = (li <= lj).astype(BF16)
    ri = lax.broadcasted_iota(I32, (rr, rr), 0)
    rj = lax.broadcasted_iota(I32, (rr, rr), 1)
    lstrict_b = (rj < ri).astype(BF16)
    rincl_b = (ri <= rj).astype(BF16)

    eqf = eq.astype(F32)
    eq_incl, _ = _cumsum_tokens(eqf, uincl_b, lstrict_b)
    sel = gt | (eq & ((eq_incl - eqf) < need))
    self_ = sel.astype(F32)
    csum, _ = _cumsum_tokens(self_, uincl_b, lstrict_b)

    ones8 = jnp.ones((SUBLANES, LANES), BF16)
    tot_lane = lax.dot_general(ones8, self_.astype(BF16), (((1,), (1,)), ((), ())),
                               preferred_element_type=F32)
    end_lane = _dot(tot_lane.astype(BF16), rincl_b)[0:1]
    beg_lane = end_lane - tot_lane[0:1]
    j = lax.broadcasted_iota(I32, (cap, 1), 0).astype(F32)
    onehot = ((beg_lane <= j) & (end_lane > j)).astype(BF16)
    row_idx = jnp.sum((end_lane <= j).astype(F32), axis=-1, keepdims=True)
    chi = jnp.floor(csum * (1.0 / LANES))
    clo = csum - chi * LANES
    crow = _dot(onehot, chi.astype(BF16)) * LANES + _dot(onehot, clo.astype(BF16))
    lane_idx = jnp.sum((crow <= j).astype(F32), axis=-1, keepdims=True)
    idx_ref[0] = (row_idx * LANES + lane_idx).astype(I32)
    ahi, amid, alo = _split3(a)
    arow = _dot(onehot, ahi) + _dot(onehot, amid) + _dot(onehot, alo)
    lane = lax.broadcasted_iota(I32, (cap, LANES), 1).astype(F32)
    g_ref[0] = jnp.sum(jnp.where(lane == lane_idx, arow, 0.0), axis=-1, keepdims=True)


def _router(aff_t, cap):
    g, rr, _ = aff_t.shape
    return pl.pallas_call(
        functools.partial(_router_body, cap=cap),
        grid=(g,),
        in_specs=[pl.BlockSpec((1, rr, LANES), lambda i: (i, 0, 0))],
        out_specs=[pl.BlockSpec((1, cap, 1), lambda i: (i, 0, 0)), pl.BlockSpec((1, cap, 1), lambda i: (i, 0, 0))],
        out_shape=[jax.ShapeDtypeStruct((g, cap, 1), I32), jax.ShapeDtypeStruct((g, cap, 1), F32)],
        compiler_params=pltpu.CompilerParams(dimension_semantics=("parallel",), vmem_limit_bytes=VMEM_LIMIT),
        name="ec_select",
    )(aff_t)


def _moe_body(idx_ref, g_ref, u_hbm, wg_ref, wu_ref, wd_ref, y_in_hbm, y_hbm, buf, xs, tmp, sem,
              *, cap, n_tok, n_experts, rc):
    del y_in_hbm
    grp = pl.program_id(0)
    h = pl.program_id(1)
    base = (grp // n_experts) * n_tok
    d = xs.shape[1]

    @pl.when(h == 0)
    def _():
        def issue(i, carry):
            for r in range(SUBLANES):
                row = base + idx_ref[0, 0, i * SUBLANES + r]
                pltpu.make_async_copy(u_hbm.at[pl.ds(row, 1)], buf.at[i, pl.ds(r, 1)], sem.at[0]).start()
            return carry
        lax.fori_loop(0, cap // SUBLANES, issue, 0)

        def drain(i, carry):
            pltpu.make_async_copy(u_hbm.at[pl.ds(0, SUBLANES)], buf.at[i], sem.at[0]).wait()
            return carry
        lax.fori_loop(0, cap // SUBLANES, drain, 0)
        xs[...] = buf[...].reshape(cap, d).astype(BF16)
        buf[...] = jnp.zeros(buf.shape, F32)

    x = xs[...]
    a = _dot(x, wg_ref[0].astype(BF16))
    b = _dot(x, wu_ref[0].astype(BF16))
    hid = (a * _sigmoid(a) * b).astype(BF16)
    buf[...] += _dot(hid, wd_ref[0].astype(BF16)).reshape(buf.shape)

    def y_rows(c, slot, to_hbm):
        def issue(i, carry):
            for r in range(SUBLANES):
                row = base + idx_ref[0, 0, c * rc + i * SUBLANES + r]
                hbm_row = y_hbm.at[pl.ds(row, 1)]
                vmem_row = tmp.at[slot, i, pl.ds(r, 1)]
                if to_hbm:
                    pltpu.make_async_copy(vmem_row, hbm_row, sem.at[3 + slot]).start()
                else:
                    pltpu.make_async_copy(hbm_row, vmem_row, sem.at[1 + slot]).start()
            return carry
        lax.fori_loop(0, rc // SUBLANES, issue, 0)

    def y_wait(slot, to_hbm):
        def drain(i, carry):
            hbm_rows = y_hbm.at[pl.ds(0, SUBLANES)]
            if to_hbm:
                pltpu.make_async_copy(tmp.at[slot, i], hbm_rows, sem.at[3 + slot]).wait()
            else:
                pltpu.make_async_copy(hbm_rows, tmp.at[slot, i], sem.at[1 + slot]).wait()
            return carry
        lax.fori_loop(0, rc // SUBLANES, drain, 0)

    @pl.when(h == pl.num_programs(1) - 1)
    def _():
        n_chunks = cap // rc
        y_rows(0, 0, False)
        for c in range(n_chunks):
            slot = c % 2
            if c + 1 < n_chunks:
                if c >= 1:
                    y_wait(1 - slot, True)
                y_rows(c + 1, 1 - slot, False)
            y_wait(slot, False)
            acc = buf[c * rc // SUBLANES:(c + 1) * rc // SUBLANES].reshape(rc, d)
            tmp[slot] = tmp[slot] + (g_ref[0, c * rc:(c + 1) * rc, :] * acc).reshape(tmp.shape[1:])
            y_rows(c, slot, True)
        for c in range(max(n_chunks - 2, 0), n_chunks):
            y_wait(c % 2, True)


def _moe(idx3, gsel, u2, w_gate, w_up, w_down, y0, n_tok, th):
    g, _, cap = idx3.shape
    n_experts, d, de = w_gate.shape
    rc = min(256, max(cap // 4, SUBLANES))
    return pl.pallas_call(
        functools.partial(_moe_body, cap=cap, n_tok=n_tok, n_experts=n_experts, rc=rc),
        grid=(g, de // th),
        in_specs=[pl.BlockSpec((1, 1, cap), lambda i, h: (i, 0, 0), memory_space=pltpu.SMEM),
                  pl.BlockSpec((1, cap, 1), lambda i, h: (i, 0, 0)),
                  pl.BlockSpec(memory_space=pl.ANY),
                  pl.BlockSpec((1, d, th), lambda i, h: (i % n_experts, 0, h)),
                  pl.BlockSpec((1, d, th), lambda i, h: (i % n_experts, 0, h)),
                  pl.BlockSpec((1, th, d), lambda i, h: (i % n_experts, h, 0)),
                  pl.BlockSpec(memory_space=pl.ANY)],
        out_specs=pl.BlockSpec(memory_space=pl.ANY),
        out_shape=jax.ShapeDtypeStruct(y0.shape, F32),
        scratch_shapes=[pltpu.VMEM((cap // SUBLANES, SUBLANES, d), F32), pltpu.VMEM((cap, d), BF16),
                        pltpu.VMEM((2, rc // SUBLANES, SUBLANES, d), F32),
                        pltpu.SemaphoreType.DMA((5,))],
        input_output_aliases={6: 0},
        compiler_params=pltpu.CompilerParams(dimension_semantics=("arbitrary", "arbitrary"),
                                             vmem_limit_bytes=VMEM_LIMIT, disable_bounds_checks=True),
        name="ec_moe",
    )(idx3, gsel, u2, w_gate, w_up, w_down, y0)


def _final_body(h1_ref, y_ref, g2_ref, w_ref, b_ref, o_ref):
    t = DEEPNORM_ALPHA * h1_ref[0] + g2_ref[0] * y_ref[0]
    o_ref[0] = _ln_rows(t) * w_ref[...] + b_ref[...]


def _final(h1, y, g2, ln2_w, ln2_b, tm):
    bsz, n, d = h1.shape
    row = lambda b, i: (b, i, 0)
    return pl.pallas_call(
        _final_body,
        grid=(bsz, n // tm),
        in_specs=[pl.BlockSpec((1, tm, d), row), pl.BlockSpec((1, tm, d), row),
                  pl.BlockSpec((1, 1, d), lambda b, i: (b, 0, 0)),
                  pl.BlockSpec((1, d), lambda b, i: (0, 0)), pl.BlockSpec((1, d), lambda b, i: (0, 0))],
        out_specs=pl.BlockSpec((1, tm, d), row),
        out_shape=jax.ShapeDtypeStruct((bsz, n, d), F32),
        compiler_params=pltpu.CompilerParams(dimension_semantics=("parallel", "parallel"),
                                             vmem_limit_bytes=VMEM_LIMIT),
        name="ln2",
    )(h1, y, g2, ln2_w.reshape(1, d), ln2_b.reshape(1, d))


def _rope_tables(n):
    rows = n // GRID_W
    pos_row = jnp.repeat(jnp.arange(rows, dtype=I32), GRID_W).astype(F32)
    pos_col = jnp.tile(jnp.arange(GRID_W, dtype=I32), rows).astype(F32)
    n_freq = DA_DQK // 4
    inv = ROPE_BASE ** (-jnp.arange(n_freq, dtype=F32) / n_freq)
    ang = jnp.concatenate([pos_row[:, None] * inv, pos_col[:, None] * inv], -1)
    cos, sin = jnp.cos(ang), jnp.sin(ang)
    reps = LANES // DA_DQK
    cosf = jnp.tile(cos, (1, 2 * reps))
    sins = jnp.tile(jnp.concatenate([-sin, sin], -1), (1, reps))
    return cosf, sins


def _pick(n, prefs):
    for t in prefs:
        if n % t == 0:
            return t
    return n


def kernel(x, c, ctx, c_ctx, w_ada, b_ada, w_in, b_gates, ml_conv_w, ml_conv_b, ml_norm_w, da_lambda, da_norm_w,
           w_out, ln1_w, ln1_b, w_router, w_gate, w_up, w_down, ln2_w, ln2_b):
    bsz, n, d = x.shape
    n_ctx = ctx.shape[1]
    n_experts = w_gate.shape[1]
    cap = EC_CAPACITY_FACTOR * n // n_experts
    l = 0
    lambda_init = 0.8 - 0.6 * math.exp(-0.3 * l)

    cin = jnp.zeros((SUBLANES, d), F32).at[:bsz].set(c).at[bsz].set(c_ctx)
    mod = _ada(cin, w_ada[l], b_ada[l])
    sh1, sc1, g1, sh2, sc2, g2 = (mod[:bsz, k * d:(k + 1) * d][:, None, :] for k in range(6))
    csh1, csc1 = (mod[bsz:bsz + 1, k * d:(k + 1) * d][:, None, :] for k in range(2))

    w = w_in[l]
    off_gates = 3 * SEC
    n_gate = 4 * N_HEADS
    w_main = jnp.concatenate([w[:, :off_gates], w[:, off_gates + n_gate:]], axis=1).astype(BF16)
    w_g = jnp.pad(w[:, off_gates:off_gates + n_gate], ((0, 0), (0, LANES - n_gate))).astype(BF16)
    b_g = jnp.pad(b_gates[l], (0, LANES - n_gate)).reshape(1, LANES)
    pml_l, pda_l, gates_l = _proj(x, sh1, sc1, w_main, w_g, b_g, _rope_tables(n), _pick(n, (512, 256, 128)))
    pml_c, pda_c, gates_c = _proj(ctx, csh1, csc1, w_main, w_g, b_g, None, _pick(n_ctx, (512, 256, 128)))

    qkc_l = _conv(pml_l, ml_conv_w[l], ml_conv_b[l], _pick(n, (512, 256, 128)))
    qkc_c = _conv(pml_c, ml_conv_w[l], ml_conv_b[l], _pick(n_ctx, (512, 256, 128)))
    gt_l = jnp.swapaxes(gates_l[:, :, :n_gate], 1, 2)
    gt_c = jnp.swapaxes(gates_c[:, :, :n_gate], 1, 2)
    c0 = jnp.zeros((bsz, 2, N_HEADS, 2 * ML_DV, 2 * ML_DQK), F32)
    m0 = jnp.zeros((bsz, 2, N_HEADS, 1, LANES), F32)
    _, _, c1, m1 = _mlstm(qkc_c, pml_c, gates_c, gt_c, c0, m0)
    hf, hb, _, _ = _mlstm(qkc_l, pml_l, gates_l, gt_l, c1, m1)

    m_all = n + n_ctx
    k_all = jnp.concatenate([pda_l[:, :, SEC:2 * SEC], pda_c[:, :, SEC:2 * SEC]], axis=1)
    v_all = jnp.concatenate([pda_l[:, :, 2 * SEC:], pda_c[:, :, 2 * SEC:]], axis=1)
    vt = v_all.reshape(bsz, m_all, N_HEADS, DA_DV).transpose(0, 2, 3, 1)
    ones_rows = jnp.zeros((bsz, N_HEADS, DA_VROWS - DA_DV, m_all), BF16).at[:, :, 0].set(1.0)
    vt = jnp.concatenate([vt, ones_rows], axis=2)
    od = _attn(da_lambda[l], pda_l, k_all, vt, _pick(n, (256, 128)), _pick(m_all, (1280, 256, 128)), lambda_init)

    w_router_pad = jnp.pad(w_router[l], ((0, 0), (0, LANES - n_experts)))
    h1, u, aff = _merge(hf, hb, pml_l, od, x, w_out[l].astype(BF16), ml_norm_w[l], da_norm_w[l], g1,
                        ln1_w[l], ln1_b[l], sh2, sc2, w_router_pad, _pick(n, (256, 128)), lambda_init, n_experts)

    aff_t = jnp.swapaxes(aff[:, :, :n_experts], 1, 2).reshape(bsz * n_experts, n // LANES, LANES)
    idx, gsel = _router(aff_t, cap)
    y = _moe(idx.reshape(bsz * n_experts, 1, cap), gsel, u.reshape(bsz * n, d), w_gate[l], w_up[l], w_down[l],
             jnp.zeros((bsz * n, d), F32), n, _pick(w_gate.shape[3], (256, 128)))
    return _final(h1, y.reshape(bsz, n, d), g2, ln2_w[l], ln2_b[l], _pick(n, (512, 256, 128)))
```

```python
import functools
import math

import jax
import jax.numpy as jnp
from jax import lax
from jax.experimental import pallas as pl
from jax.experimental.pallas import tpu as pltpu

F32 = jnp.float32
BF16 = jnp.bfloat16
I32 = jnp.int32

LANES = 128
SUBLANES = 8
GRID_W = 64
N_HEADS = 8
ML_DQK = 64
ML_DV = 128
ML_CONV_W = 5
ML_CHUNK = 128
DA_DQK = 64
DA_DV = 128
SEC = 1024
ROPE_BASE = 10000.0
LN_EPS = 1e-5
DEPTH = 1
DEEPNORM_ALPHA = (2.0 * DEPTH) ** 0.25
EC_CAPACITY_FACTOR = 2
VMEM_LIMIT = 56 * 1024 * 1024
DA_QSCALE = DA_DQK ** -0.5 * math.log2(math.e)
DA_VROWS = DA_DV + 16


def _sigmoid(x):
    return 1.0 / (1.0 + jnp.exp(-x))


def _log_sigmoid(x):
    return jnp.minimum(x, 0.0) - jnp.log(1.0 + jnp.exp(-jnp.abs(x)))


def _ln_rows(x):
    mu = jnp.mean(x, axis=-1, keepdims=True)
    xc = x - mu
    var = jnp.mean(xc * xc, axis=-1, keepdims=True)
    return xc * lax.rsqrt(var + LN_EPS)


def _split3(x):
    hi = x.astype(BF16)
    r1 = x - hi.astype(F32)
    mid = r1.astype(BF16)
    lo = (r1 - mid.astype(F32)).astype(BF16)
    return hi, mid, lo


def _dot(a, b):
    return jnp.dot(a, b, preferred_element_type=F32)


def _dot_exact_lhs(a_bf16, x_f32):
    hi, mid, lo = _split3(x_f32)
    return _dot(a_bf16, hi) + _dot(a_bf16, mid) + _dot(a_bf16, lo)


def _dot_exact_rhs(x_f32, a_bf16):
    hi, mid, lo = _split3(x_f32)
    return _dot(hi, a_bf16) + _dot(mid, a_bf16) + _dot(lo, a_bf16)


def _ada_body(c_ref, w_ref, b_ref, o_ref):
    c = c_ref[...]
    a = c * _sigmoid(c)
    hi, mid, lo = _split3(a)
    w = w_ref[...]
    whi, wmid, wlo = _split3(w)
    acc = _dot(hi, whi) + _dot(hi, wmid) + _dot(mid, whi)
    acc = acc + _dot(mid, wmid) + _dot(hi, wlo) + _dot(lo, whi)
    o_ref[...] = acc + b_ref[...]


def _ada(cin, w, b):
    rows, d = cin.shape
    n = w.shape[1]
    tn = _pick(n, (1024, 512, 256, 128))
    return pl.pallas_call(
        _ada_body,
        grid=(n // tn,),
        in_specs=[pl.BlockSpec((rows, d), lambda j: (0, 0)),
                  pl.BlockSpec((d, tn), lambda j: (0, j)),
                  pl.BlockSpec((1, tn), lambda j: (0, j))],
        out_specs=pl.BlockSpec((rows, tn), lambda j: (0, j)),
        out_shape=jax.ShapeDtypeStruct((rows, n), F32),
        compiler_params=pltpu.CompilerParams(dimension_semantics=("arbitrary",),
                                             vmem_limit_bytes=VMEM_LIMIT),
        name="ada_mod",
    )(cin, w, b.reshape(1, n))


def _rope_tile(acc, cosf, sins):
    outs = []
    lane = lax.broadcasted_iota(I32, (acc.shape[0], LANES), 1)
    first = (lane & (DA_DQK - 1)) < (DA_DQK // 2)
    for t in range(acc.shape[1] // LANES):
        a = acc[:, t * LANES:(t + 1) * LANES]
        partner = jnp.where(first, pltpu.roll(a, LANES - DA_DQK // 2, 1), pltpu.roll(a, DA_DQK // 2, 1))
        outs.append(a * cosf + partner * sins)
    return jnp.concatenate(outs, axis=1)


def _proj_body(*refs, rope):
    if rope:
        (x_ref, sh_ref, sc_ref, w_ref, wg_ref, bg_ref, cos_ref, sin_ref,
         pml_ref, pda_ref, g_ref, xn_ref) = refs
    else:
        (x_ref, sh_ref, sc_ref, w_ref, wg_ref, bg_ref, pml_ref, pda_ref, g_ref, xn_ref) = refs
    j = pl.program_id(2)

    @pl.when(j == 0)
    def _():
        y = _ln_rows(x_ref[0]) * (1.0 + sc_ref[0]) + sh_ref[0]
        yb = y.astype(BF16)
        xn_ref[...] = yb
        g_ref[0] = _dot(yb, wg_ref[...]) + bg_ref[...]

    acc = _dot(xn_ref[...], w_ref[...])

    @pl.when(j < 3)
    def _():
        pml_ref[0] = acc

    @pl.when(j == 3)
    def _():
        q = _rope_tile(acc, cos_ref[...], sin_ref[...]) if rope else acc
        pda_ref[0] = (q * DA_QSCALE).astype(BF16)

    @pl.when(j == 4)
    def _():
        k = _rope_tile(acc, cos_ref[...], sin_ref[...]) if rope else acc
        pda_ref[0] = k.astype(BF16)

    @pl.when(j == 5)
    def _():
        pda_ref[0] = acc.astype(BF16)


def _proj(x, sh, sc, w_main, w_g, b_g, rope_tabs, tm):
    bsz, n, d = x.shape
    per_batch = sh.shape[0] == bsz
    rope = rope_tabs is not None
    mod_map = (lambda b, i, j: (b, 0, 0)) if per_batch else (lambda b, i, j: (0, 0, 0))
    in_specs = [pl.BlockSpec((1, tm, d), lambda b, i, j: (b, i, 0)),
                pl.BlockSpec((1, 1, d), mod_map),
                pl.BlockSpec((1, 1, d), mod_map),
                pl.BlockSpec((d, SEC), lambda b, i, j: (0, j)),
                pl.BlockSpec((d, LANES), lambda b, i, j: (0, 0)),
                pl.BlockSpec((1, LANES), lambda b, i, j: (0, 0))]
    args = [x, sh, sc, w_main, w_g, b_g]
    if rope:
        in_specs += [pl.BlockSpec((tm, LANES), lambda b, i, j: (i, 0)),
                     pl.BlockSpec((tm, LANES), lambda b, i, j: (i, 0))]
        args += list(rope_tabs)
    return pl.pallas_call(
        functools.partial(_proj_body, rope=rope),
        grid=(bsz, n // tm, 6),
        in_specs=in_specs,
        out_specs=[pl.BlockSpec((1, tm, SEC), lambda b, i, j: (b, i, jnp.minimum(j, 2))),
                   pl.BlockSpec((1, tm, SEC), lambda b, i, j: (b, i, jnp.maximum(j - 3, 0))),
                   pl.BlockSpec((1, tm, LANES), lambda b, i, j: (b, i, 0))],
        out_shape=[jax.ShapeDtypeStruct((bsz, n, 3 * SEC), F32),
                   jax.ShapeDtypeStruct((bsz, n, 3 * SEC), BF16),
                   jax.ShapeDtypeStruct((bsz, n, LANES), F32)],
        scratch_shapes=[pltpu.VMEM((tm, d), BF16)],
        compiler_params=pltpu.CompilerParams(dimension_semantics=("parallel", "parallel", "arbitrary"),
                                             vmem_limit_bytes=VMEM_LIMIT),
        name="in_proj_rope" if rope else "in_proj",
    )(*args)


def _conv_body(prev_ref, cur_ref, next_ref, w_ref, b_ref, s_ref, o_ref):
    i = pl.program_id(1)
    last = pl.num_programs(1) - 1
    cur = cur_ref[0]
    t = cur.shape[0]
    prev = jnp.where(i > 0, prev_ref[0], 0.0)
    nxt = jnp.where(i < last, next_ref[0], 0.0)
    ext = jnp.concatenate([prev, cur, nxt], axis=0)
    acc = jnp.zeros_like(cur) + b_ref[...]
    half = ML_CONV_W // 2
    for k in range(ML_CONV_W):
        off = SUBLANES - half + k
        acc = acc + ext[off:off + t] * w_ref[k:k + 1, :]
    y = acc * _sigmoid(acc)
    o_ref[0] = (y * s_ref[...]).astype(BF16)


def _conv(p_ml, conv_w, conv_b, tt):
    bsz, n, _ = p_ml.shape
    nb8 = n // SUBLANES
    r = tt // SUBLANES
    scale = jnp.concatenate([jnp.full((1, SEC // 2), ML_DQK ** -0.5, F32), jnp.ones((1, SEC // 2), F32)], axis=1)
    return pl.pallas_call(
        _conv_body,
        grid=(bsz, n // tt),
        in_specs=[pl.BlockSpec((1, SUBLANES, SEC), lambda b, i: (b, jnp.maximum(i * r - 1, 0), 0)),
                  pl.BlockSpec((1, tt, SEC), lambda b, i: (b, i, 0)),
                  pl.BlockSpec((1, SUBLANES, SEC), lambda b, i: (b, jnp.minimum((i + 1) * r, nb8 - 1), 0)),
                  pl.BlockSpec((ML_CONV_W, SEC), lambda b, i: (0, 0)),
                  pl.BlockSpec((1, SEC), lambda b, i: (0, 0)),
                  pl.BlockSpec((1, SEC), lambda b, i: (0, 0))],
        out_specs=pl.BlockSpec((1, tt, SEC), lambda b, i: (b, i, 0)),
        out_shape=jax.ShapeDtypeStruct((bsz, n, SEC), BF16),
        compiler_params=pltpu.CompilerParams(dimension_semantics=("parallel", "parallel"),
                                             vmem_limit_bytes=VMEM_LIMIT),
        name="ml_conv",
    )(p_ml, p_ml, p_ml, conv_w, conv_b.reshape(1, SEC), scale)


def _mlstm_head(kp, qtm, vt_ext, r_col, b_row, i_row, b_tot, mask_t, c_prev, m_prev):
    dmat = jnp.where(mask_t, b_row + r_col, -jnp.inf)
    m_inter = b_row + m_prev
    m_t = jnp.maximum(m_inter, jnp.max(dmat, axis=0, keepdims=True))
    w_inter = jnp.exp(m_inter - m_t)
    w = jnp.exp(dmat - m_t) * _dot(kp, qtm)
    ext = w_inter * _dot(c_prev.astype(BF16), qtm) + _dot(vt_ext, w.astype(BF16))
    ht = ext[:ML_DV] / jnp.maximum(jnp.abs(ext[ML_DV:ML_DV + 1]), jnp.exp(-m_t))
    gdec = b_tot - b_row + i_row
    m_new = jnp.maximum(b_tot + m_prev, jnp.max(gdec, axis=-1, keepdims=True))
    a = jnp.exp(b_tot + m_prev - m_new)
    vw = (vt_ext.astype(F32) * jnp.exp(gdec - m_new)).astype(BF16)
    c_new = a * c_prev + _dot(vw, kp)
    return jnp.transpose(ht), c_new, m_new


def _mlstm_body(qf_ref, kf_ref, vf_ref, gf_ref, gtf_ref, qb_ref, kb_ref, vb_ref, gb_ref, gtb_ref,
                c0_ref, m0_ref, hf_ref, hb_ref, c_ref, m_ref):
    @pl.when(pl.program_id(1) == 0)
    def _():
        c_ref[...] = c0_ref[...]
        m_ref[...] = m0_ref[...]

    L = ML_CHUNK
    r = lax.broadcasted_iota(I32, (L, L), 0)
    s = lax.broadcasted_iota(I32, (L, L), 1)
    mle = r <= s
    mge = r >= s
    mle_b = mle.astype(BF16)
    mge_b = mge.astype(BF16)
    ones_rows = (lax.broadcasted_iota(I32, (ML_DV, L), 0) == 0).astype(BF16)
    pair_row = lax.broadcasted_iota(I32, (2 * ML_DQK, L), 0) // ML_DQK
    dirs = ((qf_ref, kf_ref, vf_ref, gf_ref, gtf_ref, hf_ref), (qb_ref, kb_ref, vb_ref, gb_ref, gtb_ref, hb_ref))
    for d, (q_ref, k_ref, v_ref, g_ref, gt_ref, h_ref) in enumerate(dirs):
        g = g_ref[0]
        gt = gt_ref[0]
        bc = _dot_exact_lhs(mge_b if d == 0 else mle_b, _log_sigmoid(g))
        br = _dot_exact_rhs(_log_sigmoid(gt), mle_b if d == 0 else mge_b)
        mask_t = mle if d == 0 else mge
        qta = jnp.transpose(q_ref[0])
        ka = k_ref[0]
        vta = jnp.transpose(v_ref[0].astype(BF16))
        outs = []
        for h in range(N_HEADS):
            ci = 2 * N_HEADS * d + h
            cf = ci + N_HEADS
            b_row = br[cf:cf + 1, :]
            b_tot = b_row[:, L - 1:L] if d == 0 else b_row[:, 0:1]
            p = h // 2
            qt_pair = qta[p * 2 * ML_DQK:(p + 1) * 2 * ML_DQK, :]
            qtm = jnp.where(pair_row == h % 2, qt_pair, jnp.zeros_like(qt_pair))
            vt_ext = jnp.concatenate([vta[h * ML_DV:(h + 1) * ML_DV, :], ones_rows], axis=0)
            outs.append(_mlstm_head(
                ka[:, p * 2 * ML_DQK:(p + 1) * 2 * ML_DQK], qtm, vt_ext,
                g[:, ci:ci + 1] - bc[:, cf:cf + 1], b_row, gt[ci:ci + 1, :], b_tot, mask_t,
                c_ref[0, d, h], m_ref[0, d, h][:, 0:1]))
        h_ref[0] = jnp.concatenate([o[0] for o in outs], axis=1)
        c_ref[0, d] = jnp.stack([o[1] for o in outs])
        m_ref[0, d] = jnp.stack([jnp.broadcast_to(o[2], (1, LANES)) for o in outs])


def _mlstm(qkc, p_ml, gates, gates_t, c0, m0):
    bsz, n, _ = qkc.shape
    L = ML_CHUNK
    nc = n // L
    hq = SEC // 2
    fwd = lambda b, i: (b, i, 0)
    bwd = lambda b, i: (b, nc - 1 - i, 0)
    st = lambda b, i: (b, 0, 0, 0, 0)
    c_spec = pl.BlockSpec((1, 2, N_HEADS, 2 * ML_DV, 2 * ML_DQK), st)
    m_spec = pl.BlockSpec((1, 2, N_HEADS, 1, LANES), st)

    def dir_specs(pos):
        return [pl.BlockSpec((1, L, hq), lambda b, i: (b, pos(i), 0)),
                pl.BlockSpec((1, L, hq), lambda b, i: (b, pos(i), 1)),
                pl.BlockSpec((1, L, SEC), lambda b, i: (b, pos(i), 1)),
                pl.BlockSpec((1, L, LANES), lambda b, i: (b, pos(i), 0)),
                pl.BlockSpec((1, 4 * N_HEADS, L), lambda b, i: (b, 0, pos(i)))]

    return pl.pallas_call(
        _mlstm_body,
        grid=(bsz, nc),
        in_specs=dir_specs(lambda i: i) + dir_specs(lambda i: nc - 1 - i) + [c_spec, m_spec],
        out_specs=[pl.BlockSpec((1, L, SEC), fwd), pl.BlockSpec((1, L, SEC), bwd), c_spec, m_spec],
        out_shape=[jax.ShapeDtypeStruct((bsz, n, SEC), F32), jax.ShapeDtypeStruct((bsz, n, SEC), F32),
                   jax.ShapeDtypeStruct(c0.shape, F32), jax.ShapeDtypeStruct(m0.shape, F32)],
        compiler_params=pltpu.CompilerParams(dimension_semantics=("parallel", "arbitrary"),
                                             vmem_limit_bytes=VMEM_LIMIT),
        name="mlstm",
    )(qkc, qkc, p_ml, gates, gates_t, qkc, qkc, p_ml, gates, gates_t, c0, m0)


def _attn_body(lam_ref, q_ref, k_ref, vt_ref, o_ref, qq_s, sa, sb, m_s, acc_s, *, tq, tk, lambda_init):
    nk = k_ref.shape[1] // tk
    nq = q_ref.shape[1] // tq
    blocks_per_iter = 1 if nk % 2 == 0 else 2
    assert nq % blocks_per_iter == 0
    bufs = (sa, sb)
    lm = lam_ref[...]
    lam = (jnp.exp(jnp.sum(lm[0:1] * lm[1:2], axis=-1, keepdims=True))
           - jnp.exp(jnp.sum(lm[2:3] * lm[3:4], axis=-1, keepdims=True)) + lambda_init)

    def scores(blk, t, s_ref):
        if t == 0:
            qoff = pl.multiple_of(jnp.minimum(blk, nq - 1) * tq, LANES)
            qt = jnp.transpose(q_ref[0, pl.ds(qoff, tq), :])
            row = lax.broadcasted_iota(I32, qt.shape, 0)
            zero = jnp.zeros_like(qt)
            qq_s[...] = jnp.concatenate([jnp.where(row < DA_DQK, qt, zero), jnp.where(row >= DA_DQK, qt, zero)],
                                        axis=1)
        s_ref[...] = _dot(k_ref[0, t * tk:(t + 1) * tk, :], qq_s[...])

    def absorb(blk, t, s_ref):
        vts = vt_ref[0, 0, :, t * tk:(t + 1) * tk]
        mt = jnp.max(s_ref[...].reshape(tk // SUBLANES, SUBLANES, 2 * tq).max(axis=0), axis=0, keepdims=True)
        if t == 0:
            m_new = mt
            acc = _dot(vts, jnp.exp2(s_ref[...] - m_new).astype(BF16))
        else:
            m_old = m_s[...]
            m_new = jnp.maximum(m_old, mt)
            p = jnp.exp2(s_ref[...] - m_new)
            acc = jnp.exp2(m_old - m_new) * acc_s[...] + _dot(vts, p.astype(BF16))
        if t == nk - 1:
            o2 = acc[:DA_DV] / acc[DA_DV:DA_DV + 1]
            qoff = pl.multiple_of(blk * tq, LANES)
            o_ref[0, pl.ds(qoff, tq), :] = jnp.transpose(o2[:, :tq] - lam * o2[:, tq:])
        else:
            acc_s[...] = acc
            m_s[...] = m_new

    scores(0, 0, bufs[0])

    def step(j, carry):
        for b in range(blocks_per_iter):
            blk = j * blocks_per_iter + b
            for t in range(nk):
                cur = (b * nk + t) % 2
                if t + 1 < nk:
                    scores(blk, t + 1, bufs[1 - cur])
                else:
                    scores(blk + 1, 0, bufs[1 - cur])
                absorb(blk, t, bufs[cur])
        return carry

    lax.fori_loop(0, nq // blocks_per_iter, step, 0)


def _attn(da_lambda, p_da, k_all, vt, tq, tk, lambda_init):
    bsz, n, _ = p_da.shape
    hd = 2 * DA_DQK
    m = k_all.shape[1]
    return pl.pallas_call(
        functools.partial(_attn_body, tq=tq, tk=tk, lambda_init=lambda_init),
        grid=(bsz, N_HEADS),
        in_specs=[pl.BlockSpec((4, DA_DQK), lambda b, h: (0, 0)),
                  pl.BlockSpec((1, n, hd), lambda b, h: (b, 0, h)),
                  pl.BlockSpec((1, m, hd), lambda b, h: (b, 0, h)),
                  pl.BlockSpec((1, 1, DA_VROWS, m), lambda b, h: (b, h, 0, 0))],
        out_specs=pl.BlockSpec((1, n, DA_DV), lambda b, h: (b, 0, h)),
        out_shape=jax.ShapeDtypeStruct((bsz, n, SEC), F32),
        scratch_shapes=[pltpu.VMEM((hd, 2 * tq), BF16),
                        pltpu.VMEM((tk, 2 * tq), F32), pltpu.VMEM((tk, 2 * tq), F32),
                        pltpu.VMEM((1, 2 * tq), F32), pltpu.VMEM((DA_VROWS, 2 * tq), F32)],
        compiler_params=pltpu.CompilerParams(dimension_semantics=("parallel", "parallel"),
                                             vmem_limit_bytes=VMEM_LIMIT),
        name="diff_attn",
    )(da_lambda, p_da, k_all, vt)


def _merge_body(hf_ref, hb_ref, o_ref, od_ref, x_ref, wout_ref, mlw_ref, daw_ref, g1_ref, ln1w_ref, ln1b_ref,
                sh2_ref, sc2_ref, wr_ref, h1_ref, u_ref, aff_ref, *, lambda_init, n_experts):
    hsum = hf_ref[0] + hb_ref[0]
    og = o_ref[0]
    od = od_ref[0]
    mlw = mlw_ref[...]
    daw = daw_ref[...]
    parts = []
    for h in range(N_HEADS):
        sl = slice(h * ML_DV, (h + 1) * ML_DV)
        parts.append(_ln_rows(hsum[:, sl]) * mlw[:, sl] * _sigmoid(og[:, sl]))
    for h in range(N_HEADS):
        sl = slice(h * DA_DV, (h + 1) * DA_DV)
        z = od[:, sl]
        zn = z * lax.rsqrt(jnp.mean(z * z, axis=-1, keepdims=True) + LN_EPS)
        parts.append(zn * daw[:, sl] * (1.0 - lambda_init))
    ycat = jnp.concatenate(parts, axis=1).astype(BF16)
    mix = _dot(ycat, wout_ref[...])
    h1 = _ln_rows(DEEPNORM_ALPHA * x_ref[0] + g1_ref[0] * mix) * ln1w_ref[...] + ln1b_ref[...]
    h1_ref[0] = h1
    u = _ln_rows(h1) * (1.0 + sc2_ref[0]) + sh2_ref[0]
    u_ref[0] = u
    uhi, umid, _ = _split3(u)
    whi, wmid, _ = _split3(wr_ref[...])
    logits = _dot(uhi, whi) + _dot(uhi, wmid) + _dot(umid, whi)
    lane = lax.broadcasted_iota(I32, logits.shape, 1)
    logits = jnp.where(lane < n_experts, logits, -jnp.inf)
    e = jnp.exp(logits - jnp.max(logits, axis=-1, keepdims=True))
    aff_ref[0] = e / jnp.sum(e, axis=-1, keepdims=True)


def _merge(hf, hb, p_ml, od, x, w_out, ml_norm_w, da_norm_w, g1, ln1_w, ln1_b, sh2, sc2, w_router_pad,
           tm, lambda_init, n_experts):
    bsz, n, d = x.shape
    row = lambda b, i: (b, i, 0)
    vec = lambda b, i: (0, 0)
    mod = lambda b, i: (b, 0, 0)
    return pl.pallas_call(
        functools.partial(_merge_body, lambda_init=lambda_init, n_experts=n_experts),
        grid=(bsz, n // tm),
        in_specs=[pl.BlockSpec((1, tm, SEC), row), pl.BlockSpec((1, tm, SEC), row),
                  pl.BlockSpec((1, tm, SEC), lambda b, i: (b, i, 2)),
                  pl.BlockSpec((1, tm, SEC), row),
                  pl.BlockSpec((1, tm, d), row),
                  pl.BlockSpec((2 * SEC, d), vec),
                  pl.BlockSpec((1, SEC), vec), pl.BlockSpec((1, SEC), vec),
                  pl.BlockSpec((1, 1, d), mod),
                  pl.BlockSpec((1, d), vec), pl.BlockSpec((1, d), vec),
                  pl.BlockSpec((1, 1, d), mod), pl.BlockSpec((1, 1, d), mod),
                  pl.BlockSpec((d, LANES), vec)],
        out_specs=[pl.BlockSpec((1, tm, d), row), pl.BlockSpec((1, tm, d), row),
                   pl.BlockSpec((1, tm, LANES), row)],
        out_shape=[jax.ShapeDtypeStruct((bsz, n, d), F32), jax.ShapeDtypeStruct((bsz, n, d), F32),
                   jax.ShapeDtypeStruct((bsz, n, LANES), F32)],
        compiler_params=pltpu.CompilerParams(dimension_semantics=("parallel", "parallel"),
                                             vmem_limit_bytes=VMEM_LIMIT),
        name="merge_ln1_router",
    )(hf, hb, p_ml, od, x, w_out, ml_norm_w.reshape(1, SEC), da_norm_w.reshape(1, SEC), g1,
      ln1_w.reshape(1, d), ln1_b.reshape(1, d), sh2, sc2, w_router_pad)


def _cumsum_tokens(maskf, uincl_b, lstrict_b):
    mb = maskf.astype(BF16)
    win = _dot(mb, uincl_b)
    tot = jnp.broadcast_to(win[:, LANES - 1:LANES], win.shape)
    offs = _dot(lstrict_b, tot.astype(BF16))
    return offs + win, tot


def _router_body(a_ref, idx_ref, g_ref, *, cap):
    a = a_ref[0]
    rr = a.shape[0]
    cur = jnp.zeros((1, 1), I32)
    for bit in range(30, -1, -1):
        cand = cur | (1 << bit)
        cnt = jnp.sum((a >= lax.bitcast_convert_type(cand, F32)).astype(I32), keepdims=True)
        cur = jnp.where(cnt >= cap, cand, cur)
    thr = lax.bitcast_convert_type(cur, F32)
    gt = a > thr
    eq = a == thr
    need = (cap - jnp.sum(gt.astype(I32), keepdims=True)).astype(F32)

    li = lax.broadcasted_iota(I32, (LANES, LANES), 0)
    lj = lax.broadcasted_iota(I32, (LANES, LANES), 1)
    uincl_b = (li <= lj).astype(BF16)
    ri = lax.broadcasted_iota(I32, (rr, rr), 0)
    rj = lax.broadcasted_iota(I32, (rr, rr), 1)
    lstrict_b = (rj < ri).astype(BF16)
    rincl_b = (ri <= rj).astype(BF16)

    eqf = eq.astype(F32)
    eq_incl, _ = _cumsum_tokens(eqf, uincl_b, lstrict_b)
    sel = gt | (eq & ((eq_incl - eqf) < need))
    self_ = sel.astype(F32)
    csum, _ = _cumsum_tokens(self_, uincl_b, lstrict_b)

    ones8 = jnp.ones((SUBLANES, LANES), BF16)
    tot_lane = lax.dot_general(ones8, self_.astype(BF16), (((1,), (1,)), ((), ())),
                               preferred_element_type=F32)
    end_lane = _dot(tot_lane.astype(BF16), rincl_b)[0:1]
    beg_lane = end_lane - tot_lane[0:1]
    j = lax.broadcasted_iota(I32, (cap, 1), 0).astype(F32)
    onehot = ((beg_lane <= j) & (end_lane > j)).astype(BF16)
    row_idx = jnp.sum((end_lane <= j).astype(F32), axis=-1, keepdims=True)
    chi = jnp.floor(csum * (1.0 / LANES))
    clo = csum - chi * LANES
    crow = _dot(onehot, chi.astype(BF16)) * LANES + _dot(onehot, clo.astype(BF16))
    lane_idx = jnp.sum((crow <= j).astype(F32), axis=-1, keepdims=True)
    idx_ref[0] = (row_idx * LANES + lane_idx).astype(I32)
    ahi, amid, alo = _split3(a)
    arow = _dot(onehot, ahi) + _dot(onehot, amid) + _dot(onehot, alo)
    lane = lax.broadcasted_iota(I32, (cap, LANES), 1).astype(F32)
    g_ref[0] = jnp.sum(jnp.where(lane == lane_idx, arow, 0.0), axis=-1, keepdims=True)


def _router(aff_t, cap):
    g, rr, _ = aff_t.shape
    return pl.pallas_call(
        functools.partial(_router_body, cap=cap),
        grid=(g,),
        in_specs=[pl.BlockSpec((1, rr, LANES), lambda i: (i, 0, 0))],
        out_specs=[pl.BlockSpec((1, cap, 1), lambda i: (i, 0, 0)), pl.BlockSpec((1, cap, 1), lambda i: (i, 0, 0))],
        out_shape=[jax.ShapeDtypeStruct((g, cap, 1), I32), jax.ShapeDtypeStruct((g, cap, 1), F32)],
        compiler_params=pltpu.CompilerParams(dimension_semantics=("parallel",), vmem_limit_bytes=VMEM_LIMIT),
        name="ec_select",
    )(aff_t)


def _moe_body(idx_ref, g_ref, u_hbm, wg_ref, wu_ref, wd_ref, y_in_hbm, y_hbm, buf, xs, tmp, sem,
              *, cap, n_tok, n_experts, rc):
    del y_in_hbm
    grp = pl.program_id(0)
    h = pl.program_id(1)
    base = (grp // n_experts) * n_tok
    d = xs.shape[1]

    @pl.when(h == 0)
    def _():
        def issue(i, carry):
            for r in range(SUBLANES):
                row = base + idx_ref[0, 0, i * SUBLANES + r]
                pltpu.make_async_copy(u_hbm.at[pl.ds(row, 1)], buf.at[i, pl.ds(r, 1)], sem.at[0]).start(priority=r % 2)
            return carry
        lax.fori_loop(0, cap // SUBLANES, issue, 0)

        def drain(i, carry):
            pltpu.make_async_copy(u_hbm.at[pl.ds(0, SUBLANES)], buf.at[i], sem.at[0]).wait()
            return carry
        lax.fori_loop(0, cap // SUBLANES, drain, 0)
        xs[...] = buf[...].reshape(cap, d).astype(BF16)
        buf[...] = jnp.zeros(buf.shape, F32)

    x = xs[...]
    a = _dot(x, wg_ref[0].astype(BF16))
    b = _dot(x, wu_ref[0].astype(BF16))
    hid = (a * _sigmoid(a) * b).astype(BF16)
    buf[...] += _dot(hid, wd_ref[0].astype(BF16)).reshape(buf.shape)

    def y_rows(c, slot, to_hbm):
        def issue(i, carry):
            for r in range(SUBLANES):
                row = base + idx_ref[0, 0, c * rc + i * SUBLANES + r]
                hbm_row = y_hbm.at[pl.ds(row, 1)]
                vmem_row = tmp.at[slot, i, pl.ds(r, 1)]
                if to_hbm:
                    pltpu.make_async_copy(vmem_row, hbm_row, sem.at[3 + slot]).start(priority=r % 2)
                else:
                    pltpu.make_async_copy(hbm_row, vmem_row, sem.at[1 + slot]).start(priority=r % 2)
            return carry
        lax.fori_loop(0, rc // SUBLANES, issue, 0)

    def y_wait(slot, to_hbm):
        def drain(i, carry):
            hbm_rows = y_hbm.at[pl.ds(0, SUBLANES)]
            if to_hbm:
                pltpu.make_async_copy(tmp.at[slot, i], hbm_rows, sem.at[3 + slot]).wait()
            else:
                pltpu.make_async_copy(hbm_rows, tmp.at[slot, i], sem.at[1 + slot]).wait()
            return carry
        lax.fori_loop(0, rc // SUBLANES, drain, 0)

    @pl.when(h == pl.num_programs(1) - 1)
    def _():
        n_chunks = cap // rc
        y_rows(0, 0, False)
        for c in range(n_chunks):
            slot = c % 2
            if c + 1 < n_chunks:
                if c >= 1:
                    y_wait(1 - slot, True)
                y_rows(c + 1, 1 - slot, False)
            y_wait(slot, False)
            acc = buf[c * rc // SUBLANES:(c + 1) * rc // SUBLANES].reshape(rc, d)
            tmp[slot] = tmp[slot] + (g_ref[0, c * rc:(c + 1) * rc, :] * acc).reshape(tmp.shape[1:])
            y_rows(c, slot, True)
        for c in range(max(n_chunks - 2, 0), n_chunks):
            y_wait(c % 2, True)


def _moe(idx3, gsel, u2, w_gate, w_up, w_down, y0, n_tok, th):
    g, _, cap = idx3.shape
    n_experts, d, de = w_gate.shape
    rc = min(256, max(cap // 4, SUBLANES))
    return pl.pallas_call(
        functools.partial(_moe_body, cap=cap, n_tok=n_tok, n_experts=n_experts, rc=rc),
        grid=(g, de // th),
        in_specs=[pl.BlockSpec((1, 1, cap), lambda i, h: (i, 0, 0), memory_space=pltpu.SMEM),
                  pl.BlockSpec((1, cap, 1), lambda i, h: (i, 0, 0)),
                  pl.BlockSpec(memory_space=pl.ANY),
                  pl.BlockSpec((1, d, th), lambda i, h: (i % n_experts, 0, h)),
                  pl.BlockSpec((1, d, th), lambda i, h: (i % n_experts, 0, h)),
                  pl.BlockSpec((1, th, d), lambda i, h: (i % n_experts, h, 0)),
                  pl.BlockSpec(memory_space=pl.ANY)],
        out_specs=pl.BlockSpec(memory_space=pl.ANY),
        out_shape=jax.ShapeDtypeStruct(y0.shape, F32),
        scratch_shapes=[pltpu.VMEM((cap // SUBLANES, SUBLANES, d), F32), pltpu.VMEM((cap, d), BF16),
                        pltpu.VMEM((2, rc // SUBLANES, SUBLANES, d), F32),
                        pltpu.SemaphoreType.DMA((5,))],
        input_output_aliases={6: 0},
        compiler_params=pltpu.CompilerParams(dimension_semantics=("arbitrary", "arbitrary"),
                                             vmem_limit_bytes=VMEM_LIMIT, disable_bounds_checks=True),
        name="ec_moe",
    )(idx3, gsel, u2, w_gate, w_up, w_down, y0)


def _final_body(h1_ref, y_ref, g2_ref, w_ref, b_ref, o_ref):
    t = DEEPNORM_ALPHA * h1_ref[0] + g2_ref[0] * y_ref[0]
    o_ref[0] = _ln_rows(t) * w_ref[...] + b_ref[...]


def _final(h1, y, g2, ln2_w, ln2_b, tm):
    bsz, n, d = h1.shape
    row = lambda b, i: (b, i, 0)
    return pl.pallas_call(
        _final_body,
        grid=(bsz, n // tm),
        in_specs=[pl.BlockSpec((1, tm, d), row), pl.BlockSpec((1, tm, d), row),
                  pl.BlockSpec((1, 1, d), lambda b, i: (b, 0, 0)),
                  pl.BlockSpec((1, d), lambda b, i: (0, 0)), pl.BlockSpec((1, d), lambda b, i: (0, 0))],
        out_specs=pl.BlockSpec((1, tm, d), row),
        out_shape=jax.ShapeDtypeStruct((bsz, n, d), F32),
        compiler_params=pltpu.CompilerParams(dimension_semantics=("parallel", "parallel"),
                                             vmem_limit_bytes=VMEM_LIMIT),
        name="ln2",
    )(h1, y, g2, ln2_w.reshape(1, d), ln2_b.reshape(1, d))


def _rope_tables(n):
    rows = n // GRID_W
    pos_row = jnp.repeat(jnp.arange(rows, dtype=I32), GRID_W).astype(F32)
    pos_col = jnp.tile(jnp.arange(GRID_W, dtype=I32), rows).astype(F32)
    n_freq = DA_DQK // 4
    inv = ROPE_BASE ** (-jnp.arange(n_freq, dtype=F32) / n_freq)
    ang = jnp.concatenate([pos_row[:, None] * inv, pos_col[:, None] * inv], -1)
    cos, sin = jnp.cos(ang), jnp.sin(ang)
    reps = LANES // DA_DQK
    cosf = jnp.tile(cos, (1, 2 * reps))
    sins = jnp.tile(jnp.concatenate([-sin, sin], -1), (1, reps))
    return cosf, sins


def _pick(n, prefs):
    for t in prefs:
        if n % t == 0:
            return t
    return n


def kernel(x, c, ctx, c_ctx, w_ada, b_ada, w_in, b_gates, ml_conv_w, ml_conv_b, ml_norm_w, da_lambda, da_norm_w,
           w_out, ln1_w, ln1_b, w_router, w_gate, w_up, w_down, ln2_w, ln2_b):
    bsz, n, d = x.shape
    n_ctx = ctx.shape[1]
    n_experts = w_gate.shape[1]
    cap = EC_CAPACITY_FACTOR * n // n_experts
    l = 0
    lambda_init = 0.8 - 0.6 * math.exp(-0.3 * l)

    cin = jnp.zeros((SUBLANES, d), F32).at[:bsz].set(c).at[bsz].set(c_ctx)
    mod = _ada(cin, w_ada[l], b_ada[l])
    sh1, sc1, g1, sh2, sc2, g2 = (mod[:bsz, k * d:(k + 1) * d][:, None, :] for k in range(6))
    csh1, csc1 = (mod[bsz:bsz + 1, k * d:(k + 1) * d][:, None, :] for k in range(2))

    w = w_in[l]
    off_gates = 3 * SEC
    n_gate = 4 * N_HEADS
    w_main = jnp.concatenate([w[:, :off_gates], w[:, off_gates + n_gate:]], axis=1).astype(BF16)
    w_g = jnp.pad(w[:, off_gates:off_gates + n_gate], ((0, 0), (0, LANES - n_gate))).astype(BF16)
    b_g = jnp.pad(b_gates[l], (0, LANES - n_gate)).reshape(1, LANES)
    pml_l, pda_l, gates_l = _proj(x, sh1, sc1, w_main, w_g, b_g, _rope_tables(n), _pick(n, (512, 256, 128)))
    pml_c, pda_c, gates_c = _proj(ctx, csh1, csc1, w_main, w_g, b_g, None, _pick(n_ctx, (512, 256, 128)))

    qkc_l = _conv(pml_l, ml_conv_w[l], ml_conv_b[l], _pick(n, (512, 256, 128)))
    qkc_c = _conv(pml_c, ml_conv_w[l], ml_conv_b[l], _pick(n_ctx, (512, 256, 128)))
    gt_l = jnp.swapaxes(gates_l[:, :, :n_gate], 1, 2)
    gt_c = jnp.swapaxes(gates_c[:, :, :n_gate], 1, 2)
    c0 = jnp.zeros((bsz, 2, N_HEADS, 2 * ML_DV, 2 * ML_DQK), F32)
    m0 = jnp.zeros((bsz, 2, N_HEADS, 1, LANES), F32)
    _, _, c1, m1 = _mlstm(qkc_c, pml_c, gates_c, gt_c, c0, m0)
    hf, hb, _, _ = _mlstm(qkc_l, pml_l, gates_l, gt_l, c1, m1)

    m_all = n + n_ctx
    k_all = jnp.concatenate([pda_l[:, :, SEC:2 * SEC], pda_c[:, :, SEC:2 * SEC]], axis=1)
    v_all = jnp.concatenate([pda_l[:, :, 2 * SEC:], pda_c[:, :, 2 * SEC:]], axis=1)
    vt = v_all.reshape(bsz, m_all, N_HEADS, DA_DV).transpose(0, 2, 3, 1)
    ones_rows = jnp.zeros((bsz, N_HEADS, DA_VROWS - DA_DV, m_all), BF16).at[:, :, 0].set(1.0)
    vt = jnp.concatenate([vt, ones_rows], axis=2)
    od = _attn(da_lambda[l], pda_l, k_all, vt, _pick(n, (256, 128)), _pick(m_all, (1280, 256, 128)), lambda_init)

    w_router_pad = jnp.pad(w_router[l], ((0, 0), (0, LANES - n_experts)))
    h1, u, aff = _merge(hf, hb, pml_l, od, x, w_out[l].astype(BF16), ml_norm_w[l], da_norm_w[l], g1,
                        ln1_w[l], ln1_b[l], sh2, sc2, w_router_pad, _pick(n, (256, 128)), lambda_init, n_experts)

    aff_t = jnp.swapaxes(aff[:, :, :n_experts], 1, 2).reshape(bsz * n_experts, n // LANES, LANES)
    idx, gsel = _router(aff_t, cap)
    y = _moe(idx.reshape(bsz * n_experts, 1, cap), gsel, u.reshape(bsz * n, d), w_gate[l], w_up[l], w_down[l],
             jnp.zeros((bsz * n, d), F32), n, _pick(w_gate.shape[3], (256, 128)))
    return _final(h1, y.reshape(bsz, n, d), g2, ln2_w[l], ln2_b[l], _pick(n, (512, 256, 128)))
```
